```python
import functools
import jax, jax.numpy as jnp
from jax import lax
import numpy as np


D_MODEL = 1024
BATCH = 8
SEQ = 2048
DEPTH = 4
DEC_BATCH = 32
DEC_SEQ = 1
PAST_LEN = 8192
PAGE_SIZE = 128

HEAD_DIM = 64
HEADS_PER_GROUP = 4
GROUPS = ((128, 1), (512, 4), (2048, 16))
N_GROUPS = len(GROUPS)
N_HEADS = HEADS_PER_GROUP * N_GROUPS
ATTN_WIDTH = N_HEADS * HEAD_DIM
CONV_WIDTH = (3 * D_MODEL) // 4
CONV_K = 31
ROT_DIM = HEAD_DIM // 4
ROPE_THETA = 500000.0
QBLK = 128
RMS_EPS = 1e-6
LN_EPS = 1e-5
IN_SIZES = (ATTN_WIDTH, ATTN_WIDTH, ATTN_WIDTH, ATTN_WIDTH,
            CONV_WIDTH, CONV_WIDTH, CONV_WIDTH, D_MODEL, D_MODEL)
IN_COLS = sum(IN_SIZES)

kernel_name = 'dilated_attn_conformer_conv_gated_hybrid_step'


def rmsnorm(x, g):
    xf = x.astype(jnp.float32)
    y = xf * lax.rsqrt(jnp.mean(xf * xf, axis=-1, keepdims=True) + RMS_EPS)
    return (y * g.astype(jnp.float32)).astype(x.dtype)


def layernorm(x, g, b):
    xf = x.astype(jnp.float32)
    mu = jnp.mean(xf, axis=-1, keepdims=True)
    xc = xf - mu
    var = jnp.mean(xc * xc, axis=-1, keepdims=True)
    y = xc * lax.rsqrt(var + LN_EPS) * g.astype(jnp.float32) + b.astype(jnp.float32)
    return y.astype(x.dtype)


def partial_rope(x, pos):
    inv = jnp.power(jnp.float32(ROPE_THETA), -jnp.arange(0, ROT_DIM, 2, dtype=jnp.float32) / ROT_DIM)
    ang = pos.astype(jnp.float32)[:, None] * inv[None, :]
    cos = jnp.cos(ang)[None, :, None, :]
    sin = jnp.sin(ang)[None, :, None, :]
    xr = x[..., :ROT_DIM].astype(jnp.float32)
    x1, x2 = xr[..., :ROT_DIM // 2], xr[..., ROT_DIM // 2:]
    rot = jnp.concatenate([x1 * cos - x2 * sin, x2 * cos + x1 * sin], axis=-1)
    return jnp.concatenate([rot.astype(x.dtype), x[..., ROT_DIM:]], axis=-1)


def group_attend(q, k_ext, v_ext, key_idx, valid):
    kg = k_ext[:, key_idx]
    vg = v_ext[:, key_idx]
    s = jnp.einsum('bqhd,bqkhd->bhqk', q, kg).astype(jnp.float32) * (HEAD_DIM ** -0.5)
    s = jnp.where(valid[None, None], s, jnp.finfo(jnp.float32).min)
    m = jnp.max(s, axis=-1, keepdims=True)
    p = jnp.exp(s - m)
    den = jnp.sum(p, axis=-1, keepdims=True)
    o = jnp.einsum('bhqk,bqkhd->bqhd', p.astype(vg.dtype), vg).astype(jnp.float32)
    o = o / jnp.transpose(den, (0, 2, 1, 3))
    lse = (m + jnp.log(den))[..., 0]
    return o, lse


def combine_groups(outs, lses, dtype):
    alpha = jax.nn.softmax(jnp.stack(lses, axis=0), axis=0)
    parts = [o * jnp.transpose(alpha[g], (0, 2, 1))[..., None] for g, o in enumerate(outs)]
    y = jnp.concatenate(parts, axis=2)
    return y.reshape(y.shape[0], y.shape[1], ATTN_WIDTH).astype(dtype)


def attn_prompt(q, k, v):
    B, S = q.shape[0], q.shape[1]
    i = jnp.arange(QBLK)
    kps, vps, states_k, states_v = [], [], [], []
    for g, (W, d) in enumerate(GROUPS):
        hs = slice(g * HEADS_PER_GROUP, (g + 1) * HEADS_PER_GROUP)
        kg, vg = k[:, :, hs], v[:, :, hs]
        kps.append(jnp.pad(kg, ((0, 0), (W, 0), (0, 0), (0, 0))))
        vps.append(jnp.pad(vg, ((0, 0), (W, 0), (0, 0), (0, 0))))
        keep = min(W, S)
        states_k.append(kg[:, S - keep:])
        states_v.append(vg[:, S - keep:])

    def block(b):
        s0 = b * QBLK
        qb = lax.dynamic_slice_in_dim(q, s0, QBLK, axis=1)
        outs, lses = [], []
        for g, (W, d) in enumerate(GROUPS):
            hs = slice(g * HEADS_PER_GROUP, (g + 1) * HEADS_PER_GROUP)
            j = jnp.arange(W // d + 1)
            kb = lax.dynamic_slice_in_dim(kps[g], s0, W + QBLK, axis=1)
            vb = lax.dynamic_slice_in_dim(vps[g], s0, W + QBLK, axis=1)
            key_idx = W + i[:, None] - d * j[None, :]
            valid = (s0 + i[:, None] - d * j[None, :]) >= 0
            o, l = group_attend(qb[:, :, hs], kb, vb, key_idx, valid)
            outs.append(o)
            lses.append(l)
        return combine_groups(outs, lses, q.dtype)

    y = lax.map(block, jnp.arange(S // QBLK))
    y = jnp.transpose(y, (1, 0, 2, 3)).reshape(B, S, ATTN_WIDTH)
    return y, (states_k, states_v)


def attn_sample(q, k, v, caches_k, caches_v):
    T = q.shape[1]
    i = jnp.arange(T)
    outs, lses, states_k, states_v = [], [], [], []
    for g, (W, d) in enumerate(GROUPS):
        hs = slice(g * HEADS_PER_GROUP, (g + 1) * HEADS_PER_GROUP)
        ck, cv = caches_k[g], caches_v[g]
        L = ck.shape[1]
        ke = jnp.concatenate([ck.astype(k.dtype), k[:, :, hs]], axis=1)
        ve = jnp.concatenate([cv.astype(v.dtype), v[:, :, hs]], axis=1)
        j = jnp.arange(W // d + 1)
        key_idx = L + i[:, None] - d * j[None, :]
        valid = key_idx >= 0
        o, l = group_attend(q[:, :, hs], ke, ve, jnp.maximum(key_idx, 0), valid)
        outs.append(o)
        lses.append(l)
        keep = min(W, L + T)
        states_k.append(ke[:, L + T - keep:])
        states_v.append(ve[:, L + T - keep:])
    return combine_groups(outs, lses, q.dtype), (states_k, states_v)


def causal_dwconv(u_ext, w):
    C = u_ext.shape[-1]
    return lax.conv_general_dilated(u_ext, w[:, None, :].astype(u_ext.dtype), window_strides=(1,),
                                    padding='VALID', dimension_numbers=('NWC', 'WIO', 'NWC'),
                                    feature_group_count=C)


def conv_prompt(u, w):
    ext = jnp.pad(u, ((0, 0), (CONV_K - 1, 0), (0, 0)))
    return causal_dwconv(ext, w), u[:, u.shape[1] - (CONV_K - 1):]


def conv_sample(u, w, buf):
    ext = jnp.concatenate([buf.astype(u.dtype), u], axis=1)
    return causal_dwconv(ext, w), ext[:, ext.shape[1] - (CONV_K - 1):]


def trunk_layer(x, pos, attn_mixer, conv_mixer, w_in, w_ao, w_co, w_o, cw, ln_g, ln_b, g_pre, g_post):
    B, S = x.shape[0], x.shape[1]
    h = rmsnorm(x, g_pre)
    z = h @ w_in
    q, k, v, ga, ca, cb, gc, ma, mc = jnp.split(z, list(np.cumsum(IN_SIZES)[:-1]), axis=-1)
    q = partial_rope(q.reshape(B, S, N_HEADS, HEAD_DIM), pos)
    k = partial_rope(k.reshape(B, S, N_HEADS, HEAD_DIM), pos)
    v = v.reshape(B, S, N_HEADS, HEAD_DIM)
    a, kv_state = attn_mixer(q, k, v)
    ya = (a * jax.nn.silu(ga)) @ w_ao
    u = ca * jax.nn.sigmoid(cb)
    cdw, conv_state = conv_mixer(u, cw)
    yc = (jax.nn.silu(layernorm(cdw, ln_g, ln_b)) * jax.nn.silu(gc)) @ w_co
    merged = jax.nn.sigmoid(ma) * ya + jax.nn.sigmoid(mc) * yc
    out = rmsnorm(merged @ w_o, g_post)
    return x + out, kv_state, conv_state


def setup_inputs(seed: int = 0) -> dict:
    key = jax.random.key(seed)
    ks = jax.random.split(key, 20)
    f32 = jnp.float32
    d = {}
    d['x_prompt'] = jax.random.normal(ks[0], (BATCH, SEQ, D_MODEL), f32)
    d['x_sample'] = jax.random.normal(ks[1], (DEC_BATCH, DEC_SEQ, D_MODEL), f32)
    for g, (W, _) in enumerate(GROUPS):
        L = min(W, PAST_LEN)
        shp = (DEPTH, DEC_BATCH, L, HEADS_PER_GROUP, HEAD_DIM)
        d['cache_k%d' % g] = jax.random.normal(ks[2 + 2 * g], shp, f32)
        d['cache_v%d' % g] = jax.random.normal(ks[3 + 2 * g], shp, f32)
    d['state_conv'] = jax.random.normal(ks[8], (DEPTH, DEC_BATCH, CONV_K - 1, CONV_WIDTH), f32)
    d['w_in'] = jax.random.normal(ks[9], (DEPTH, D_MODEL, IN_COLS), f32) * D_MODEL ** -0.5
    d['w_attn_out'] = jax.random.normal(ks[10], (DEPTH, ATTN_WIDTH, D_MODEL), f32) * ATTN_WIDTH ** -0.5
    d['w_conv_out'] = jax.random.normal(ks[11], (DEPTH, CONV_WIDTH, D_MODEL), f32) * CONV_WIDTH ** -0.5
    d['w_out'] = jax.random.normal(ks[12], (DEPTH, D_MODEL, D_MODEL), f32) * D_MODEL ** -0.5
    d['conv_w'] = jax.random.normal(ks[13], (DEPTH, CONV_K, CONV_WIDTH), f32) * CONV_K ** -0.5
    d['conv_ln_g'] = 1.0 + 0.02 * jax.random.normal(ks[14], (DEPTH, CONV_WIDTH), f32)
    d['conv_ln_b'] = 0.02 * jax.random.normal(ks[15], (DEPTH, CONV_WIDTH), f32)
    d['norm_pre'] = 1.0 + 0.02 * jax.random.normal(ks[16], (DEPTH, D_MODEL), f32)
    d['norm_post'] = 1.0 + 0.02 * jax.random.normal(ks[17], (DEPTH, D_MODEL), f32)
    return d


def reference(x_prompt, x_sample, cache_k0, cache_v0, cache_k1, cache_v1, cache_k2, cache_v2,
              state_conv, w_in, w_attn_out, w_conv_out, w_out, conv_w, conv_ln_g, conv_ln_b,
              norm_pre, norm_post):
    pos_p = jnp.arange(x_prompt.shape[1], dtype=jnp.int32)
    pos_s = PAST_LEN + jnp.arange(x_sample.shape[1], dtype=jnp.int32)
    yp, ys = x_prompt, x_sample
    pk = [[] for _ in GROUPS]
    pv = [[] for _ in GROUPS]
    sk = [[] for _ in GROUPS]
    sv = [[] for _ in GROUPS]
    pc, sc = [], []
    for l in range(DEPTH):
        params = (w_in[l], w_attn_out[l], w_conv_out[l], w_out[l], conv_w[l],
                  conv_ln_g[l], conv_ln_b[l], norm_pre[l], norm_post[l])
        yp, (kst, vst), cst = trunk_layer(yp, pos_p, attn_prompt, conv_prompt, *params)
        for g in range(N_GROUPS):
            pk[g].append(kst[g])
            pv[g].append(vst[g])
        pc.append(cst)
        s_attn = functools.partial(attn_sample,
                                   caches_k=(cache_k0[l], cache_k1[l], cache_k2[l]),
                                   caches_v=(cache_v0[l], cache_v1[l], cache_v2[l]))
        s_conv = functools.partial(conv_sample, buf=state_conv[l])
        ys, (kst, vst), cst = trunk_layer(ys, pos_s, s_attn, s_conv, *params)
        for g in range(N_GROUPS):
            sk[g].append(kst[g])
            sv[g].append(vst[g])
        sc.append(cst)
    return (yp, ys,
            jnp.stack(pk[0]), jnp.stack(pv[0]), jnp.stack(pk[1]), jnp.stack(pv[1]),
            jnp.stack(pk[2]), jnp.stack(pv[2]), jnp.stack(pc),
            jnp.stack(sk[0]), jnp.stack(sv[0]), jnp.stack(sk[1]), jnp.stack(sv[1]),
            jnp.stack(sk[2]), jnp.stack(sv[2]), jnp.stack(sc))
```

```python
import functools

import numpy as np
import jax
import jax.numpy as jnp
from jax import lax
from jax.experimental import pallas as pl
from jax.experimental.pallas import tpu as pltpu

D_MODEL = 1024
HEAD_DIM = 64
HEADS_PER_GROUP = 4
GROUPS = ((128, 1), (512, 4), (2048, 16))
N_GROUPS = len(GROUPS)
GROUP_WIDTH = HEADS_PER_GROUP * HEAD_DIM
ATTN_WIDTH = N_GROUPS * GROUP_WIDTH
CONV_WIDTH = 768
CONV_K = 31
ROT_DIM = 16
ROPE_THETA = 500000.0
QBLK = 128
RMS_EPS = 1e-6
LN_EPS = 1e-5
PAST_LEN = 8192
IN_COLS = 4 * ATTN_WIDTH + 3 * CONV_WIDTH + 2 * D_MODEL

COL_Q, COL_K, COL_V, COL_GA = 0, 768, 1536, 2304
COL_CA, COL_CB, COL_GC = 3072, 3840, 4608
COL_MA, COL_MC = 5376, 6400

CONV_HALO = 32
VMEM_LIMIT = 56 * 1024 * 1024

F32 = jnp.float32
BF16 = jnp.bfloat16
F32_MIN = float(jnp.finfo(jnp.float32).min)
F32_MAX = float(jnp.finfo(jnp.float32).max)


def _sigmoid(x):
    return jax.nn.sigmoid(x)


def _silu(x):
    return x * jax.nn.sigmoid(x)


def _inproj_kernel(x_ref, g_ref, w_ref, cos_ref, sa_ref, sb_ref,
                   q0_ref, q1_ref, q2_ref, k0_ref, k1_ref, k2_ref, v0_ref, v1_ref, v2_ref,
                   kf_ref, vf_ref, ga_ref, u_ref, gc_ref, ma_ref, mc_ref):
    x = x_ref[...]
    ms = jnp.mean(x * x, axis=-1, keepdims=True)
    h = (x * lax.rsqrt(ms + RMS_EPS)) * g_ref[...]
    hb = h.astype(BF16)

    def proj(c0):
        return jnp.dot(hb, w_ref[:, c0:c0 + GROUP_WIDTH], preferred_element_type=F32)

    cos = cos_ref[...]
    sa = sa_ref[...]
    sb = sb_ref[...]

    def rope(z):
        return (z * cos + pltpu.roll(z, GROUP_WIDTH - ROT_DIM // 2, 1) * sa
                + pltpu.roll(z, ROT_DIM // 2, 1) * sb)

    q_refs = (q0_ref, q1_ref, q2_ref)
    k_refs = (k0_ref, k1_ref, k2_ref)
    v_refs = (v0_ref, v1_ref, v2_ref)
    for g in range(N_GROUPS):
        cs = slice(g * GROUP_WIDTH, (g + 1) * GROUP_WIDTH)
        zq = rope(proj(COL_Q + g * GROUP_WIDTH))
        q_refs[g][...] = (zq * (HEAD_DIM ** -0.5)).astype(BF16)
        zk = rope(proj(COL_K + g * GROUP_WIDTH))
        k_refs[g][...] = zk.astype(BF16)
        kf_ref[:, cs] = zk
        zv = proj(COL_V + g * GROUP_WIDTH)
        v_refs[g][...] = zv.astype(BF16)
        vf_ref[:, cs] = zv
    for c in range(CONV_WIDTH // GROUP_WIDTH):
        cs = slice(c * GROUP_WIDTH, (c + 1) * GROUP_WIDTH)
        ga_ref[:, cs] = _silu(proj(COL_GA + c * GROUP_WIDTH)).astype(BF16)
        a = proj(COL_CA + c * GROUP_WIDTH)
        b = proj(COL_CB + c * GROUP_WIDTH)
        u_ref[:, cs] = a * _sigmoid(b)
        gc_ref[:, cs] = _silu(proj(COL_GC + c * GROUP_WIDTH)).astype(BF16)
    for c in range(D_MODEL // GROUP_WIDTH):
        cs = slice(c * GROUP_WIDTH, (c + 1) * GROUP_WIDTH)
        ma_ref[:, cs] = _sigmoid(proj(COL_MA + c * GROUP_WIDTH)).astype(BF16)
        mc_ref[:, cs] = _sigmoid(proj(COL_MC + c * GROUP_WIDTH)).astype(BF16)


def _inproj(x2d, g_pre, w_in_bf, cos, sa, sb, tm):
    m = x2d.shape[0]
    n_tab = cos.shape[0] // tm
    row = lambda i: (i, 0)
    full = lambda i: (0, 0)
    tab = lambda i: (i % n_tab, 0)
    gw = pl.BlockSpec((tm, GROUP_WIDTH), row)
    in_specs = [
        pl.BlockSpec((tm, D_MODEL), row),
        pl.BlockSpec((1, D_MODEL), full),
        pl.BlockSpec((D_MODEL, IN_COLS), full, pipeline_mode=pl.Buffered(1)),
        pl.BlockSpec((tm, GROUP_WIDTH), tab),
        pl.BlockSpec((tm, GROUP_WIDTH), tab),
        pl.BlockSpec((tm, GROUP_WIDTH), tab),
    ]
    out_shape = ([jax.ShapeDtypeStruct((m, GROUP_WIDTH), BF16)] * 9
                 + [jax.ShapeDtypeStruct((m, ATTN_WIDTH), F32)] * 2
                 + [jax.ShapeDtypeStruct((m, ATTN_WIDTH), BF16),
                    jax.ShapeDtypeStruct((m, CONV_WIDTH), F32),
                    jax.ShapeDtypeStruct((m, CONV_WIDTH), BF16),
                    jax.ShapeDtypeStruct((m, D_MODEL), BF16),
                    jax.ShapeDtypeStruct((m, D_MODEL), BF16)])
    out_specs = ([gw] * 9
                 + [pl.BlockSpec((tm, ATTN_WIDTH), row)] * 2
                 + [pl.BlockSpec((tm, ATTN_WIDTH), row),
                    pl.BlockSpec((tm, CONV_WIDTH), row),
                    pl.BlockSpec((tm, CONV_WIDTH), row),
                    pl.BlockSpec((tm, D_MODEL), row),
                    pl.BlockSpec((tm, D_MODEL), row)])
    return pl.pallas_call(
        _inproj_kernel,
        grid=(m // tm,),
        in_specs=in_specs,
        out_specs=out_specs,
        out_shape=out_shape,
        compiler_params=pltpu.CompilerParams(
            dimension_semantics=("parallel",), vmem_limit_bytes=VMEM_LIMIT),
        name="inproj",
    )(x2d, g_pre, w_in_bf, cos, sa, sb)


def _head_masks():
    lane = lax.broadcasted_iota(jnp.int32, (1, GROUP_WIDTH), 1)
    return [lax.shift_right_logical(lane, 6) == h for h in range(HEADS_PER_GROUP)]


def _attend_block(q, k, v, cap, hmask):
    zero = jnp.zeros_like(q)
    qs = jnp.concatenate([jnp.where(hmask[h], q, zero) for h in range(HEADS_PER_GROUP)], axis=0)
    s = lax.dot_general(qs, k, (((1,), (1,)), ((), ())), preferred_element_type=F32)
    s = jnp.minimum(s, cap)
    m = jnp.max(s, axis=-1, keepdims=True)
    p = jnp.exp(s - m)
    den = jnp.sum(p, axis=-1, keepdims=True)
    r = jnp.dot(p.astype(BF16), v, preferred_element_type=F32)
    r = r / den
    lse = m + jnp.log(den)
    o = jnp.zeros((QBLK, GROUP_WIDTH), F32)
    l = jnp.zeros((QBLK, GROUP_WIDTH), F32)
    for h in range(HEADS_PER_GROUP):
        rows = slice(h * QBLK, (h + 1) * QBLK)
        o = jnp.where(hmask[h], r[rows], o)
        l = jnp.where(hmask[h], jnp.broadcast_to(lse[rows], (QBLK, GROUP_WIDTH)), l)
    return o, l


def _attn_kernel(q0_ref, k0_ref, v0_ref, q1_ref, k1_ref, v1_ref, q2_ref, k2_ref, v2_ref,
                 o0_ref, l0_ref, o1_ref, l1_ref, o2_ref, l2_ref, cap_band_ref, cap_first_ref):
    hmask = _head_masks()
    row = jnp.bitwise_and(lax.broadcasted_iota(jnp.int32, (4 * QBLK, 2 * QBLK), 0), QBLK - 1)
    col = lax.broadcasted_iota(jnp.int32, (4 * QBLK, 2 * QBLK), 1)
    cap_band_ref[...] = jnp.where((col >= row) & (col <= row + QBLK), F32_MAX, F32_MIN)
    row1 = jnp.bitwise_and(lax.broadcasted_iota(jnp.int32, (4 * QBLK, QBLK), 0), QBLK - 1)
    col1 = lax.broadcasted_iota(jnp.int32, (4 * QBLK, QBLK), 1)
    cap_first_ref[...] = jnp.where(col1 <= row1, F32_MAX, F32_MIN)

    def run_group(q_ref, k_ref, v_ref, o_ref, l_ref, dil):
        seq = q_ref.shape[0]
        nblk = seq // QBLK
        for r in range(dil):
            cs = slice(r * GROUP_WIDTH, (r + 1) * GROUP_WIDTH)
            o, l = _attend_block(q_ref[0:QBLK, cs], k_ref[0:QBLK, cs], v_ref[0:QBLK, cs],
                                 cap_first_ref[...], hmask)
            o_ref[0:QBLK, cs] = o.astype(BF16)
            l_ref[0:QBLK, cs] = l
            if nblk > 1:
                def body(j, carry):
                    qo = pl.multiple_of(j * QBLK, QBLK)
                    ko = pl.multiple_of((j - 1) * QBLK, QBLK)
                    ob, lb = _attend_block(q_ref[pl.ds(qo, QBLK), cs], k_ref[pl.ds(ko, 2 * QBLK), cs],
                                           v_ref[pl.ds(ko, 2 * QBLK), cs], cap_band_ref[...], hmask)
                    o_ref[pl.ds(qo, QBLK), cs] = ob.astype(BF16)
                    l_ref[pl.ds(qo, QBLK), cs] = lb
                    return carry
                lax.fori_loop(1, nblk, body, 0)

    run_group(q0_ref, k0_ref, v0_ref, o0_ref, l0_ref, GROUPS[0][1])
    run_group(q1_ref, k1_ref, v1_ref, o1_ref, l1_ref, GROUPS[1][1])
    run_group(q2_ref, k2_ref, v2_ref, o2_ref, l2_ref, GROUPS[2][1])


def _attention(qkv, batch, seq):
    ins, in_specs, out_shape, out_specs = [], [], [], []
    for g, (_, dil) in enumerate(GROUPS):
        shp = (batch, seq // dil, dil * GROUP_WIDTH)
        spec = pl.BlockSpec((None, seq // dil, dil * GROUP_WIDTH), lambda b: (b, 0, 0))
        for t in range(3):
            ins.append(qkv[3 * g + t].reshape(shp))
            in_specs.append(spec)
        out_shape += [jax.ShapeDtypeStruct(shp, BF16), jax.ShapeDtypeStruct(shp, F32)]
        out_specs += [spec, spec]
    outs = pl.pallas_call(
        _attn_kernel,
        grid=(batch,),
        in_specs=in_specs,
        out_specs=out_specs,
        out_shape=out_shape,
        scratch_shapes=[pltpu.VMEM((4 * QBLK, 2 * QBLK), F32), pltpu.VMEM((4 * QBLK, QBLK), F32)],
        compiler_params=pltpu.CompilerParams(
            dimension_semantics=("parallel",), vmem_limit_bytes=VMEM_LIMIT),
        name="dilated_attn",
    )(*ins)
    m = batch * seq
    o = [outs[2 * g].reshape(m, GROUP_WIDTH) for g in range(N_GROUPS)]
    l = [outs[2 * g + 1].reshape(m, GROUP_WIDTH) for g in range(N_GROUPS)]
    return o, l


def _sattn_kernel(q0_ref, q1_ref, q2_ref, kn0_ref, kn1_ref, kn2_ref, vn0_ref, vn1_ref, vn2_ref,
                  ck0_ref, cv0_ref, ck1_ref, cv1_ref, ck2_ref, cv2_ref,
                  o0_ref, l0_ref, o1_ref, l1_ref, o2_ref, l2_ref):
    ri = lax.shift_right_logical(lax.broadcasted_iota(jnp.int32, (GROUP_WIDTH, GROUP_WIDTH), 0), 6)
    ci = lax.shift_right_logical(lax.broadcasted_iota(jnp.int32, (GROUP_WIDTH, GROUP_WIDTH), 1), 6)
    seg = jnp.where(ri == ci, 1.0, 0.0).astype(BF16)

    def head_sum(prod):
        hi = prod.astype(BF16)
        lo = (prod - hi.astype(F32)).astype(BF16)
        return (jnp.dot(hi, seg, preferred_element_type=F32)
                + jnp.dot(lo, seg, preferred_element_type=F32))

    def group(q_ref, kn_ref, vn_ref, ck_ref, cv_ref, o_ref, l_ref):
        nb, nk, _ = ck_ref.shape
        q = q_ref[...].astype(F32)
        kn = kn_ref[...].astype(F32)
        vn = vn_ref[...].astype(F32)
        ck = ck_ref[...].astype(BF16).astype(F32)
        s = head_sum((ck * q[:, None, :]).reshape(nb * nk, GROUP_WIDTH)).reshape(nb, nk, GROUP_WIDTH)
        s_new = head_sum(kn * q)
        m = jnp.maximum(jnp.max(s, axis=1), s_new)
        p = jnp.exp(s - m[:, None, :])
        p_new = jnp.exp(s_new - m)
        den = jnp.sum(p, axis=1) + p_new
        cv = cv_ref[...].astype(BF16).astype(F32)
        acc = (jnp.sum(p.astype(BF16).astype(F32) * cv, axis=1)
               + p_new.astype(BF16).astype(F32) * vn)
        o_ref[...] = (acc / den).astype(BF16)
        l_ref[...] = m + jnp.log(den)

    group(q0_ref, kn0_ref, vn0_ref, ck0_ref, cv0_ref, o0_ref, l0_ref)
    group(q1_ref, kn1_ref, vn1_ref, ck1_ref, cv1_ref, o1_ref, l1_ref)
    group(q2_ref, kn2_ref, vn2_ref, ck2_ref, cv2_ref, o2_ref, l2_ref)


def _sample_attention(qkv, caches, layer, nb_blk):
    db = qkv[0].shape[0]
    row = pl.BlockSpec((nb_blk, GROUP_WIDTH), lambda i: (i, 0))
    ins = [qkv[0], qkv[3], qkv[6], qkv[1], qkv[4], qkv[7], qkv[2], qkv[5], qkv[8]]
    in_specs = [row] * 9
    for g, (win, dil) in enumerate(GROUPS):
        for t in range(2):
            c = caches[2 * g + t]
            depth, _, length = c.shape[:3]
            ins.append(c.reshape(depth, db, length // dil, dil * GROUP_WIDTH))
            in_specs.append(pl.BlockSpec((None, nb_blk, length // dil, GROUP_WIDTH),
                                         lambda i, layer=layer: (layer, i, 0, 0)))
    out_shape = [jax.ShapeDtypeStruct((db, GROUP_WIDTH), BF16),
                 jax.ShapeDtypeStruct((db, GROUP_WIDTH), F32)] * N_GROUPS
    outs = pl.pallas_call(
        _sattn_kernel,
        grid=(db // nb_blk,),
        in_specs=in_specs,
        out_specs=[row] * (2 * N_GROUPS),
        out_shape=out_shape,
        compiler_params=pltpu.CompilerParams(
            dimension_semantics=("parallel",), vmem_limit_bytes=VMEM_LIMIT),
        name="sample_attn",
    )(*ins)
    return [outs[0], outs[2], outs[4]], [outs[1], outs[3], outs[5]]


def _post_tail(o_refs, l_refs, ga_ref, cdw, gc_ref, ma_ref, mc_ref, x_ref,
               wao_ref, wco_ref, wo_ref, lng_ref, lnb_ref, gpost_ref, y_ref):
    ls = [r[...] for r in l_refs]
    mx = jnp.maximum(jnp.maximum(ls[0], ls[1]), ls[2])
    es = [jnp.exp(l - mx) for l in ls]
    inv = 1.0 / (es[0] + es[1] + es[2])
    a = jnp.concatenate([o_refs[g][...].astype(F32) * (es[g] * inv) for g in range(N_GROUPS)], axis=1)
    ya = jnp.dot((a * ga_ref[...].astype(F32)).astype(BF16), wao_ref[...], preferred_element_type=F32)

    mu = jnp.mean(cdw, axis=-1, keepdims=True)
    xc = cdw - mu
    var = jnp.mean(xc * xc, axis=-1, keepdims=True)
    yn = xc * lax.rsqrt(var + LN_EPS) * lng_ref[...] + lnb_ref[...]
    ct = _silu(yn) * gc_ref[...].astype(F32)
    yc = jnp.dot(ct.astype(BF16), wco_ref[...], preferred_element_type=F32)

    merged = ma_ref[...].astype(F32) * ya + mc_ref[...].astype(F32) * yc
    z = jnp.dot(merged.astype(BF16), wo_ref[...], preferred_element_type=F32)
    ms = jnp.mean(z * z, axis=-1, keepdims=True)
    y_ref[...] = x_ref[...] + (z * lax.rsqrt(ms + RMS_EPS)) * gpost_ref[...]


def _post_prompt_kernel(o0_ref, o1_ref, o2_ref, l0_ref, l1_ref, l2_ref, ga_ref, u_ref, up_ref,
                        gc_ref, ma_ref, mc_ref, x_ref, wao_ref, wco_ref, wo_ref, cw_ref,
                        lng_ref, lnb_ref, gpost_ref, y_ref, ext_ref, cdw_ref, *, conv_rows):
    tm = u_ref.shape[0]
    ext_ref[CONV_HALO:, :] = u_ref[...]

    @pl.when(pl.program_id(1) == 0)
    def _():
        ext_ref[0:CONV_HALO, :] = jnp.zeros((CONV_HALO, CONV_WIDTH), F32)

    @pl.when(pl.program_id(1) > 0)
    def _():
        ext_ref[0:CONV_HALO, :] = up_ref[...]

    base = CONV_HALO - (CONV_K - 1)
    for c0 in range(0, tm, conv_rows):
        acc = jnp.zeros((conv_rows, CONV_WIDTH), F32)
        for j in range(CONV_K):
            acc = acc + cw_ref[j:j + 1, :] * ext_ref[c0 + base + j:c0 + base + j + conv_rows, :]
        cdw_ref[c0:c0 + conv_rows, :] = acc

    _post_tail((o0_ref, o1_ref, o2_ref), (l0_ref, l1_ref, l2_ref), ga_ref, cdw_ref[...], gc_ref,
               ma_ref, mc_ref, x_ref, wao_ref, wco_ref, wo_ref, lng_ref, lnb_ref, gpost_ref, y_ref)


def _post_sample_kernel(o0_ref, o1_ref, o2_ref, l0_ref, l1_ref, l2_ref, ga_ref, u_ref, st_ref,
                        gc_ref, ma_ref, mc_ref, x_ref, wao_ref, wco_ref, wo_ref, cw_ref,
                        lng_ref, lnb_ref, gpost_ref, y_ref):
    cw = cw_ref[...]
    cdw = (jnp.sum(st_ref[...] * cw[None, 0:CONV_K - 1, :], axis=1)
           + cw[CONV_K - 1:CONV_K, :] * u_ref[...])
    _post_tail((o0_ref, o1_ref, o2_ref), (l0_ref, l1_ref, l2_ref), ga_ref, cdw, gc_ref,
               ma_ref, mc_ref, x_ref, wao_ref, wco_ref, wo_ref, lng_ref, lnb_ref, gpost_ref, y_ref)


def _weight_specs(nd):
    full2 = (lambda *i: (0, 0))
    del nd
    return [
        pl.BlockSpec((ATTN_WIDTH, D_MODEL), full2, pipeline_mode=pl.Buffered(1)),
        pl.BlockSpec((CONV_WIDTH, D_MODEL), full2, pipeline_mode=pl.Buffered(1)),
        pl.BlockSpec((D_MODEL, D_MODEL), full2, pipeline_mode=pl.Buffered(1)),
        pl.BlockSpec((CONV_K, CONV_WIDTH), full2),
        pl.BlockSpec((1, CONV_WIDTH), full2),
        pl.BlockSpec((1, CONV_WIDTH), full2),
        pl.BlockSpec((1, D_MODEL), full2),
    ]


def _post_prompt(o, l, ga, u, gc, ma, mc, x2d, weights, batch, seq, tm):
    m = batch * seq
    nt = seq // tm
    row = lambda b, i: (b * nt + i, 0)
    halo = lambda b, i: (jnp.maximum((b * nt + i) * (tm // CONV_HALO) - 1, 0), 0)
    gw = pl.BlockSpec((tm, GROUP_WIDTH), row)
    in_specs = ([gw] * 6
                + [pl.BlockSpec((tm, ATTN_WIDTH), row),
                   pl.BlockSpec((tm, CONV_WIDTH), row),
                   pl.BlockSpec((CONV_HALO, CONV_WIDTH), halo),
                   pl.BlockSpec((tm, CONV_WIDTH), row),
                   pl.BlockSpec((tm, D_MODEL), row),
                   pl.BlockSpec((tm, D_MODEL), row),
                   pl.BlockSpec((tm, D_MODEL), row)]
                + _weight_specs(2))
    return pl.pallas_call(
        functools.partial(_post_prompt_kernel, conv_rows=32),
        grid=(batch, nt),
        in_specs=in_specs,
        out_specs=pl.BlockSpec((tm, D_MODEL), row),
        out_shape=jax.ShapeDtypeStruct((m, D_MODEL), F32),
        scratch_shapes=[pltpu.VMEM((tm + CONV_HALO, CONV_WIDTH), F32),
                        pltpu.VMEM((tm, CONV_WIDTH), F32)],
        compiler_params=pltpu.CompilerParams(
            dimension_semantics=("parallel", "parallel"), vmem_limit_bytes=VMEM_LIMIT),
        name="post_prompt",
    )(*o, *l, ga, u, u, gc, ma, mc, x2d, *weights)


def _post_sample(o, l, ga, u, state, gc, ma, mc, x2d, weights, layer):
    db = x2d.shape[0]
    row = lambda i: (0, 0)
    gw = pl.BlockSpec((db, GROUP_WIDTH), row)
    in_specs = ([gw] * 6
                + [pl.BlockSpec((db, ATTN_WIDTH), row),
                   pl.BlockSpec((db, CONV_WIDTH), row),
                   pl.BlockSpec((None, db, CONV_K - 1, CONV_WIDTH), lambda i, layer=layer: (layer, 0, 0, 0)),
                   pl.BlockSpec((db, CONV_WIDTH), row),
                   pl.BlockSpec((db, D_MODEL), row),
                   pl.BlockSpec((db, D_MODEL), row),
                   pl.BlockSpec((db, D_MODEL), row)]
                + _weight_specs(1))
    return pl.pallas_call(
        _post_sample_kernel,
        grid=(1,),
        in_specs=in_specs,
        out_specs=pl.BlockSpec((db, D_MODEL), row),
        out_shape=jax.ShapeDtypeStruct((db, D_MODEL), F32),
        compiler_params=pltpu.CompilerParams(
            dimension_semantics=("arbitrary",), vmem_limit_bytes=VMEM_LIMIT),
        name="post_sample",
    )(*o, *l, ga, u, state, gc, ma, mc, x2d, *weights)


def _rope_tables(pos):
    inv = jnp.power(jnp.float32(ROPE_THETA), -jnp.arange(0, ROT_DIM, 2, dtype=F32) / ROT_DIM)
    ang = pos.astype(F32)[:, None] * inv[None, :]
    cos, sin = jnp.cos(ang), jnp.sin(ang)
    npos = pos.shape[0]
    half = ROT_DIM // 2
    pad = jnp.zeros((npos, HEAD_DIM - ROT_DIM), F32)
    zero = jnp.zeros((npos, half), F32)
    cos_h = jnp.concatenate([cos, cos, pad + 1.0], axis=1)
    sa_h = jnp.concatenate([-sin, zero, pad], axis=1)
    sb_h = jnp.concatenate([zero, sin, pad], axis=1)
    rep = lambda t: jnp.tile(t, (1, HEADS_PER_GROUP))
    return rep(cos_h), rep(sa_h), rep(sb_h)


def kernel(x_prompt, x_sample, cache_k0, cache_v0, cache_k1, cache_v1, cache_k2, cache_v2, state_conv,
           w_in, w_attn_out, w_conv_out, w_out, conv_w, conv_ln_g, conv_ln_b, norm_pre, norm_post):
    batch, seq, _ = x_prompt.shape
    db, dseq, _ = x_sample.shape
    depth = w_in.shape[0]
    assert dseq == 1 and seq % 512 == 0 and seq >= GROUPS[-1][0]
    caches = (cache_k0, cache_v0, cache_k1, cache_v1, cache_k2, cache_v2)
    for g, (win, _) in enumerate(GROUPS):
        assert caches[2 * g].shape[2] == win

    w_in_bf = w_in.astype(BF16)
    w_ao_bf = w_attn_out.astype(BF16)
    w_co_bf = w_conv_out.astype(BF16)
    w_o_bf = w_out.astype(BF16)

    tab_p = _rope_tables(jnp.arange(seq, dtype=jnp.int32))
    tab_s = _rope_tables(jnp.full((db,), PAST_LEN, dtype=jnp.int32))

    tm_in, tm_post = 512, 256
    yp = x_prompt.reshape(batch * seq, D_MODEL)
    ys = x_sample.reshape(db, D_MODEL)
    pk = [[] for _ in GROUPS]
    pv = [[] for _ in GROUPS]
    sk = [[] for _ in GROUPS]
    sv = [[] for _ in GROUPS]
    pc, sc = [], []
    for layer in range(depth):
        g_pre = norm_pre[layer][None, :]
        weights = (w_ao_bf[layer], w_co_bf[layer], w_o_bf[layer], conv_w[layer],
                   conv_ln_g[layer][None, :], conv_ln_b[layer][None, :], norm_post[layer][None, :])

        outs = _inproj(yp, g_pre, w_in_bf[layer], *tab_p, tm_in)
        q, k, v = outs[0:3], outs[3:6], outs[6:9]
        kf, vf, ga, u, gc, ma, mc = outs[9:]
        qkv = [t for g in range(N_GROUPS) for t in (q[g], k[g], v[g])]
        o, l = _attention(qkv, batch, seq)
        yp = _post_prompt(o, l, ga, u, gc, ma, mc, yp, weights, batch, seq, tm_post)
        kf3 = kf.reshape(batch, seq, ATTN_WIDTH)
        vf3 = vf.reshape(batch, seq, ATTN_WIDTH)
        for g, (win, _) in enumerate(GROUPS):
            keep = min(win, seq)
            cs = slice(g * GROUP_WIDTH, (g + 1) * GROUP_WIDTH)
            pk[g].append(kf3[:, seq - keep:, cs].reshape(batch, keep, HEADS_PER_GROUP, HEAD_DIM))
            pv[g].append(vf3[:, seq - keep:, cs].reshape(batch, keep, HEADS_PER_GROUP, HEAD_DIM))
        pc.append(u.reshape(batch, seq, CONV_WIDTH)[:, seq - (CONV_K - 1):])

        outs = _inproj(ys, g_pre, w_in_bf[layer], *tab_s, db)
        q, k, v = outs[0:3], outs[3:6], outs[6:9]
        kf, vf, ga, u, gc, ma, mc = outs[9:]
        qkv = [t for g in range(N_GROUPS) for t in (q[g], k[g], v[g])]
        o, l = _sample_attention(qkv, caches, layer, 16)
        ys = _post_sample(o, l, ga, u, state_conv, gc, ma, mc, ys, weights, layer)
        for g in range(N_GROUPS):
            cs = slice(g * GROUP_WIDTH, (g + 1) * GROUP_WIDTH)
            new_k = kf[:, cs].reshape(db, 1, HEADS_PER_GROUP, HEAD_DIM)
            new_v = vf[:, cs].reshape(db, 1, HEADS_PER_GROUP, HEAD_DIM)
            sk[g].append(jnp.concatenate([caches[2 * g][layer][:, 1:], new_k], axis=1))
            sv[g].append(jnp.concatenate([caches[2 * g + 1][layer][:, 1:], new_v], axis=1))
        sc.append(jnp.concatenate([state_conv[layer][:, 1:], u[:, None, :]], axis=1))

    return (yp.reshape(batch, seq, D_MODEL), ys.reshape(db, 1, D_MODEL),
            jnp.stack(pk[0]), jnp.stack(pv[0]), jnp.stack(pk[1]), jnp.stack(pv[1]),
            jnp.stack(pk[2]), jnp.stack(pv[2]), jnp.stack(pc),
            jnp.stack(sk[0]), jnp.stack(sv[0]), jnp.stack(sk[1]), jnp.stack(sv[1]),
            jnp.stack(sk[2]), jnp.stack(sv[2]), jnp.stack(sc))
```

```python
import functools

import jax
import jax.numpy as jnp
from jax import lax
from jax.experimental import pallas as pl
from jax.experimental.pallas import tpu as pltpu

D_MODEL = 1024
HEAD_DIM = 64
HEADS_PER_GROUP = 4
GROUPS = ((128, 1), (512, 4), (2048, 16))
N_GROUPS = len(GROUPS)
GROUP_WIDTH = HEADS_PER_GROUP * HEAD_DIM
ATTN_WIDTH = N_GROUPS * GROUP_WIDTH
CONV_WIDTH = 768
CONV_K = 31
ROT_DIM = 16
ROPE_THETA = 500000.0
QBLK = 128
RMS_EPS = 1e-6
LN_EPS = 1e-5
PAST_LEN = 8192
IN_COLS = 4 * ATTN_WIDTH + 3 * CONV_WIDTH + 2 * D_MODEL

COL_Q, COL_K, COL_V, COL_GA = 0, 768, 1536, 2304
COL_CA, COL_CB, COL_GC = 3072, 3840, 4608
COL_MA, COL_MC = 5376, 6400

SUBLANES = 8
LANES = 128
CONV_HALO = 32
CONV_BASE = CONV_HALO - (CONV_K - 1)
CONV_CHUNK = 32
TM_IN = 512
TM_POST = 256
VMEM_LIMIT = 56 * 1024 * 1024

F32 = jnp.float32
BF16 = jnp.bfloat16
F32_MIN = float(jnp.finfo(jnp.float32).min)
F32_MAX = float(jnp.finfo(jnp.float32).max)


def _sigmoid(x):
    return jax.nn.sigmoid(x)


def _silu(x):
    return x * jax.nn.sigmoid(x)


def _round_bf16(x):
    return x.astype(BF16).astype(F32)


def _inproj_body(x_ref, g_ref, w_ref, cos_ref, sa_ref, sb_ref, emit_q, emit_k, emit_v,
                 ga_ref, u_ref, gc_ref, ma_ref, mc_ref):
    x = x_ref[...]
    ms = jnp.mean(x * x, axis=-1, keepdims=True)
    h = (x * lax.rsqrt(ms + RMS_EPS)) * g_ref[...]
    hb = h.astype(BF16)

    def proj(c0):
        return jnp.dot(hb, w_ref[:, c0:c0 + GROUP_WIDTH], preferred_element_type=F32)

    cos = cos_ref[...]
    sa = sa_ref[...]
    sb = sb_ref[...]

    def rope(z):
        return (z * cos + pltpu.roll(z, GROUP_WIDTH - ROT_DIM // 2, 1) * sa
                + pltpu.roll(z, ROT_DIM // 2, 1) * sb)

    for g in range(N_GROUPS):
        emit_q(g, rope(proj(COL_Q + g * GROUP_WIDTH)) * (HEAD_DIM ** -0.5))
        emit_k(g, rope(proj(COL_K + g * GROUP_WIDTH)))
        emit_v(g, proj(COL_V + g * GROUP_WIDTH))
    for c in range(CONV_WIDTH // GROUP_WIDTH):
        cs = slice(c * GROUP_WIDTH, (c + 1) * GROUP_WIDTH)
        ga_ref[:, cs] = _silu(proj(COL_GA + c * GROUP_WIDTH)).astype(BF16)
        a = proj(COL_CA + c * GROUP_WIDTH)
        b = proj(COL_CB + c * GROUP_WIDTH)
        u_ref[:, cs] = a * _sigmoid(b)
        gc_ref[:, cs] = _silu(proj(COL_GC + c * GROUP_WIDTH)).astype(BF16)
    for c in range(D_MODEL // GROUP_WIDTH):
        cs = slice(c * GROUP_WIDTH, (c + 1) * GROUP_WIDTH)
        ma_ref[:, cs] = _sigmoid(proj(COL_MA + c * GROUP_WIDTH)).astype(BF16)
        mc_ref[:, cs] = _sigmoid(proj(COL_MC + c * GROUP_WIDTH)).astype(BF16)


def _inproj_prompt_kernel(*refs, n_alias):
    x_ref, g_ref, w_ref, cos_ref, sa_ref, sb_ref = refs[:6]
    outs = refs[6 + n_alias:]
    qkv_refs = outs[0:9]
    pk_refs = outs[9:12]
    pv_refs = outs[12:15]
    ga_ref, u_ref, gc_ref, ma_ref, mc_ref, tmp_ref = outs[15:21]
    tm = x_ref.shape[0]

    def emit_decimated(ref, z, dil):
        if dil == 1:
            ref[...] = z.astype(BF16)
            return
        rows = tm // dil
        for half in range(GROUP_WIDTH // LANES):
            tmp_ref[half] = z[:, half * LANES:(half + 1) * LANES]
        for r in range(dil):
            for half in range(GROUP_WIDTH // LANES):
                c0 = r * GROUP_WIDTH + half * LANES
                ref[:, c0:c0 + LANES] = tmp_ref[half, pl.ds(r, rows, stride=dil), :].astype(BF16)

    def emit_state(ref, z):
        keep = ref.shape[1]
        ref[...] = z[tm - keep:, :].T

    def emit_q(g, z):
        emit_decimated(qkv_refs[3 * g], z, GROUPS[g][1])

    def emit_k(g, z):
        emit_decimated(qkv_refs[3 * g + 1], z, GROUPS[g][1])
        emit_state(pk_refs[g], z)

    def emit_v(g, z):
        emit_decimated(qkv_refs[3 * g + 2], z, GROUPS[g][1])
        emit_state(pv_refs[g], z)

    _inproj_body(x_ref, g_ref, w_ref, cos_ref, sa_ref, sb_ref, emit_q, emit_k, emit_v,
                 ga_ref, u_ref, gc_ref, ma_ref, mc_ref)


def _inproj_sample_kernel(x_ref, g_ref, w_ref, cos_ref, sa_ref, sb_ref,
                          q_ref, kf_ref, vf_ref, ga_ref, u_ref, gc_ref, ma_ref, mc_ref):
    def emit_q(g, z):
        q_ref[:, g * GROUP_WIDTH:(g + 1) * GROUP_WIDTH] = _round_bf16(z)

    def emit_k(g, z):
        kf_ref[:, g * GROUP_WIDTH:(g + 1) * GROUP_WIDTH] = z

    def emit_v(g, z):
        vf_ref[:, g * GROUP_WIDTH:(g + 1) * GROUP_WIDTH] = z

    _inproj_body(x_ref, g_ref, w_ref, cos_ref, sa_ref, sb_ref, emit_q, emit_k, emit_v,
                 ga_ref, u_ref, gc_ref, ma_ref, mc_ref)


def _inproj_in_specs(tm, n_tab):
    row = lambda i: (i, 0)
    full = lambda i: (0, 0)
    tab = lambda i: (i % n_tab, 0)
    return [
        pl.BlockSpec((tm, D_MODEL), row),
        pl.BlockSpec((1, D_MODEL), full),
        pl.BlockSpec((D_MODEL, IN_COLS), full, pipeline_mode=pl.Buffered(1)),
        pl.BlockSpec((tm, GROUP_WIDTH), tab),
        pl.BlockSpec((tm, GROUP_WIDTH), tab),
        pl.BlockSpec((tm, GROUP_WIDTH), tab),
    ]


def _gate_out(m, tm):
    row = lambda i: (i, 0)
    shapes = [jax.ShapeDtypeStruct((m, ATTN_WIDTH), BF16),
              jax.ShapeDtypeStruct((m, CONV_WIDTH), F32),
              jax.ShapeDtypeStruct((m, CONV_WIDTH), BF16),
              jax.ShapeDtypeStruct((m, D_MODEL), BF16),
              jax.ShapeDtypeStruct((m, D_MODEL), BF16)]
    specs = [pl.BlockSpec((tm, ATTN_WIDTH), row),
             pl.BlockSpec((tm, CONV_WIDTH), row),
             pl.BlockSpec((tm, CONV_WIDTH), row),
             pl.BlockSpec((tm, D_MODEL), row),
             pl.BlockSpec((tm, D_MODEL), row)]
    return shapes, specs


def _inproj_prompt(x2d, g_pre, w_in_bf, tabs, batch, seq, depth, layer, prev_states):
    tm = TM_IN
    m = batch * seq
    nt = seq // tm
    in_specs = _inproj_in_specs(tm, nt)
    n_alias = len(prev_states)
    in_specs += [pl.BlockSpec(memory_space=pl.ANY)] * n_alias

    out_shape, out_specs = [], []
    for _, dil in GROUPS:
        shp = (batch, seq // dil, dil * GROUP_WIDTH)
        spec = pl.BlockSpec((None, tm // dil, dil * GROUP_WIDTH), lambda i: (i // nt, i % nt, 0))
        out_shape += [jax.ShapeDtypeStruct(shp, BF16)] * 3
        out_specs += [spec] * 3
    for _ in range(2):
        for win, _ in GROUPS:
            keep = min(win, seq)
            blk = min(keep, tm)
            nblk = keep // blk
            imap = (lambda i, nblk=nblk:
                    (layer, i // nt, 0, jnp.maximum(i % nt - (nt - nblk), 0)))
            out_shape.append(jax.ShapeDtypeStruct((depth, batch, GROUP_WIDTH, keep), F32))
            out_specs.append(pl.BlockSpec((None, None, GROUP_WIDTH, blk), imap))
    gs, gp = _gate_out(m, tm)
    out_shape += gs
    out_specs += gp
    aliases = {6 + j: 9 + j for j in range(n_alias)}
    outs = pl.pallas_call(
        functools.partial(_inproj_prompt_kernel, n_alias=n_alias),
        grid=(m // tm,),
        in_specs=in_specs,
        out_specs=out_specs,
        out_shape=out_shape,
        scratch_shapes=[pltpu.VMEM((GROUP_WIDTH // LANES, tm, LANES), F32)],
        input_output_aliases=aliases,
        compiler_params=pltpu.CompilerParams(
            dimension_semantics=("arbitrary",), vmem_limit_bytes=VMEM_LIMIT),
        name="inproj_prompt",
    )(x2d, g_pre, w_in_bf, *tabs, *prev_states)
    return outs[0:9], outs[9:15], outs[15:20]


def _inproj_sample(x2d, g_pre, w_in_bf, tabs):
    m = x2d.shape[0]
    row = lambda i: (i, 0)
    gs, gp = _gate_out(m, m)
    out_shape = [jax.ShapeDtypeStruct((m, ATTN_WIDTH), F32)] * 3 + gs
    out_specs = [pl.BlockSpec((m, ATTN_WIDTH), row)] * 3 + gp
    outs = pl.pallas_call(
        _inproj_sample_kernel,
        grid=(1,),
        in_specs=_inproj_in_specs(m, 1),
        out_specs=out_specs,
        out_shape=out_shape,
        compiler_params=pltpu.CompilerParams(
            dimension_semantics=("arbitrary",), vmem_limit_bytes=VMEM_LIMIT),
        name="inproj_sample",
    )(x2d, g_pre, w_in_bf, *tabs)
    return outs[0:3], outs[3:8]


def _head_masks():
    lane = lax.broadcasted_iota(jnp.int32, (1, GROUP_WIDTH), 1)
    return [lax.shift_right_logical(lane, 6) == h for h in range(HEADS_PER_GROUP)]


def _attend_block(q, k, v, cap, hmask):
    zero = jnp.zeros_like(q)
    qs = jnp.concatenate([jnp.where(hmask[h], q, zero) for h in range(HEADS_PER_GROUP)], axis=0)
    s = lax.dot_general(qs, k, (((1,), (1,)), ((), ())), preferred_element_type=F32)
    s = jnp.minimum(s, cap)
    m = jnp.max(s, axis=-1, keepdims=True)
    p = jnp.exp(s - m)
    den = jnp.sum(p, axis=-1, keepdims=True)
    r = jnp.dot(p.astype(BF16), v, preferred_element_type=F32)
    r = r / den
    lse = m + jnp.log(den)
    o = jnp.zeros((QBLK, GROUP_WIDTH), F32)
    l = jnp.zeros((QBLK, GROUP_WIDTH), F32)
    for h in range(HEADS_PER_GROUP):
        rows = slice(h * QBLK, (h + 1) * QBLK)
        o = jnp.where(hmask[h], r[rows], o)
        l = jnp.where(hmask[h], jnp.broadcast_to(lse[rows], (QBLK, GROUP_WIDTH)), l)
    return o, l


def _attn_kernel(q0_ref, k0_ref, v0_ref, q1_ref, k1_ref, v1_ref, q2_ref, k2_ref, v2_ref,
                 o0_ref, l0_ref, o1_ref, l1_ref, o2_ref, l2_ref, cap_band_ref, cap_first_ref):
    hmask = _head_masks()
    row = jnp.bitwise_and(lax.broadcasted_iota(jnp.int32, (4 * QBLK, 2 * QBLK), 0), QBLK - 1)
    col = lax.broadcasted_iota(jnp.int32, (4 * QBLK, 2 * QBLK), 1)
    cap_band_ref[...] = jnp.where((col >= row) & (col <= row + QBLK), F32_MAX, F32_MIN)
    row1 = jnp.bitwise_and(lax.broadcasted_iota(jnp.int32, (4 * QBLK, QBLK), 0), QBLK - 1)
    col1 = lax.broadcasted_iota(jnp.int32, (4 * QBLK, QBLK), 1)
    cap_first_ref[...] = jnp.where(col1 <= row1, F32_MAX, F32_MIN)

    def run_group(q_ref, k_ref, v_ref, o_ref, l_ref, dil):
        seq = q_ref.shape[0]
        nblk = seq // QBLK
        for r in range(dil):
            cs = slice(r * GROUP_WIDTH, (r + 1) * GROUP_WIDTH)
            o, l = _attend_block(q_ref[0:QBLK, cs], k_ref[0:QBLK, cs], v_ref[0:QBLK, cs],
                                 cap_first_ref[...], hmask)
            o_ref[0:QBLK, cs] = o
            l_ref[0:QBLK, cs] = l
            if nblk > 1:
                def body(j, carry):
                    qo = pl.multiple_of(j * QBLK, QBLK)
                    ko = pl.multiple_of((j - 1) * QBLK, QBLK)
                    ob, lb = _attend_block(q_ref[pl.ds(qo, QBLK), cs], k_ref[pl.ds(ko, 2 * QBLK), cs],
                                           v_ref[pl.ds(ko, 2 * QBLK), cs], cap_band_ref[...], hmask)
                    o_ref[pl.ds(qo, QBLK), cs] = ob
                    l_ref[pl.ds(qo, QBLK), cs] = lb
                    return carry
                lax.fori_loop(1, nblk, body, 0)

    run_group(q0_ref, k0_ref, v0_ref, o0_ref, l0_ref, GROUPS[0][1])
    run_group(q1_ref, k1_ref, v1_ref, o1_ref, l1_ref, GROUPS[1][1])
    run_group(q2_ref, k2_ref, v2_ref, o2_ref, l2_ref, GROUPS[2][1])


def _attention(qkv, batch, seq):
    in_specs, out_shape, out_specs = [], [], []
    for _, dil in GROUPS:
        shp = (batch, seq // dil, dil * GROUP_WIDTH)
        spec = pl.BlockSpec((None, seq // dil, dil * GROUP_WIDTH), lambda b: (b, 0, 0))
        in_specs += [spec] * 3
        out_shape += [jax.ShapeDtypeStruct(shp, F32)] * 2
        out_specs += [spec, spec]
    outs = pl.pallas_call(
        _attn_kernel,
        grid=(batch,),
        in_specs=in_specs,
        out_specs=out_specs,
        out_shape=out_shape,
        scratch_shapes=[pltpu.VMEM((4 * QBLK, 2 * QBLK), F32), pltpu.VMEM((4 * QBLK, QBLK), F32)],
        compiler_params=pltpu.CompilerParams(
            dimension_semantics=("parallel",), vmem_limit_bytes=VMEM_LIMIT),
        name="dilated_attn",
    )(*qkv)
    return [outs[0], outs[2], outs[4]], [outs[1], outs[3], outs[5]]


def _sattn_kernel(*refs, n_alias):
    q_ref, kf_ref, vf_ref = refs[0:3]
    cache_refs = refs[3:9]
    outs = refs[9 + n_alias:]
    o_ref, l_ref = outs[0:2]
    state_refs = outs[2:8]

    b = pl.program_id(0)
    qrow = q_ref[pl.ds(b, 1), :]
    krow = kf_ref[pl.ds(b, 1), :]
    vrow = vf_ref[pl.ds(b, 1), :]

    ri = lax.broadcasted_iota(jnp.int32, (GROUP_WIDTH, GROUP_WIDTH), 0)
    ci = lax.broadcasted_iota(jnp.int32, (GROUP_WIDTH, GROUP_WIDTH), 1)
    diag = ri == ci
    lane_head = lax.shift_right_logical(lax.broadcasted_iota(jnp.int32, (1, GROUP_WIDTH), 1), 6)

    def to_col(row):
        return jnp.sum(jnp.where(diag, jnp.broadcast_to(row, (GROUP_WIDTH, GROUP_WIDTH)), 0.0),
                       axis=1, keepdims=True)

    def to_row(col):
        return jnp.sum(jnp.where(diag, jnp.broadcast_to(col, (GROUP_WIDTH, GROUP_WIDTH)), 0.0),
                       axis=0, keepdims=True)

    def head_sum(x):
        return jnp.sum(x.reshape(HEADS_PER_GROUP, HEAD_DIM, x.shape[1]), axis=1)

    def head_row(x):
        out = jnp.zeros((1, GROUP_WIDTH), F32)
        for h in range(HEADS_PER_GROUP):
            out = jnp.where(lane_head == h, jnp.broadcast_to(x[h:h + 1, :], (1, GROUP_WIDTH)), out)
        return out

    for g, (win, dil) in enumerate(GROUPS):
        cs = slice(g * GROUP_WIDTH, (g + 1) * GROUP_WIDTH)
        ck_ref, cv_ref = cache_refs[2 * g], cache_refs[2 * g + 1]
        sk_ref, sv_ref = state_refs[2 * g], state_refs[2 * g + 1]
        length = ck_ref.shape[1]
        qc = to_col(qrow[:, cs])
        kc = to_col(krow[:, cs])
        vc = to_col(vrow[:, cs])

        kmat = ck_ref[...]
        s = head_sum(_round_bf16(kmat) * qc)
        lane = lax.broadcasted_iota(jnp.int32, (HEADS_PER_GROUP, length), 1)
        s = jnp.where(jnp.bitwise_and(lane, dil - 1) == 0, s, F32_MIN)
        s_new = head_sum(jnp.broadcast_to(_round_bf16(kc) * qc, (GROUP_WIDTH, LANES)))[:, 0:1]
        m = jnp.maximum(jnp.max(s, axis=1, keepdims=True), s_new)
        p = jnp.exp(s - m)
        p_new = jnp.exp(s_new - m)
        den = jnp.sum(p, axis=1, keepdims=True) + p_new

        vmat = cv_ref[...]
        pb = _round_bf16(p)
        p_full = jnp.broadcast_to(pb[:, None, :], (HEADS_PER_GROUP, HEAD_DIM, length)).reshape(GROUP_WIDTH, length)
        acc_c = jnp.sum(p_full * _round_bf16(vmat), axis=1, keepdims=True)
        acc_row = to_row(acc_c) + head_row(_round_bf16(p_new)) * _round_bf16(vrow[:, cs])
        den_row = head_row(den)
        o_ref[pl.ds(b, 1), cs] = acc_row / den_row
        l_ref[pl.ds(b, 1), cs] = head_row(m) + jnp.log(den_row)

        last = lax.broadcasted_iota(jnp.int32, (GROUP_WIDTH, length), 1) == length - 1
        sk_ref[...] = jnp.where(last, kc, pltpu.roll(kmat, length - 1, 1))
        sv_ref[...] = jnp.where(last, vc, pltpu.roll(vmat, length - 1, 1))


def _sample_attention(q, kf, vf, caches_t, layer, prev_states):
    db = q.shape[0]
    depth = caches_t[0].shape[0]
    n_alias = len(prev_states)
    full = pl.BlockSpec((db, ATTN_WIDTH), lambda b: (0, 0))
    in_specs = [full] * 3
    out_shape = [jax.ShapeDtypeStruct((db, ATTN_WIDTH), F32)] * 2
    out_specs = [full, full]
    for c in caches_t:
        length = c.shape[3]
        spec = pl.BlockSpec((None, None, GROUP_WIDTH, length), lambda b: (layer, b, 0, 0))
        in_specs.append(spec)
        out_shape.append(jax.ShapeDtypeStruct((depth, db, GROUP_WIDTH, length), F32))
        out_specs.append(spec)
    in_specs += [pl.BlockSpec(memory_space=pl.ANY)] * n_alias
    aliases = {9 + j: 2 + j for j in range(n_alias)}
    outs = pl.pallas_call(
        functools.partial(_sattn_kernel, n_alias=n_alias),
        grid=(db,),
        in_specs=in_specs,
        out_specs=out_specs,
        out_shape=out_shape,
        input_output_aliases=aliases,
        compiler_params=pltpu.CompilerParams(
            dimension_semantics=("arbitrary",), vmem_limit_bytes=VMEM_LIMIT),
        name="sample_attn",
    )(q, kf, vf, *caches_t, *prev_states)
    return outs[0], outs[1], outs[2:8]


def _post_tail(os, ls, ga_ref, cdw, gc_ref, ma_ref, mc_ref, x_ref,
               wao_ref, wco_ref, wo_ref, lng_ref, lnb_ref, gpost_ref, y_ref):
    mx = jnp.maximum(jnp.maximum(ls[0], ls[1]), ls[2])
    es = [jnp.exp(l - mx) for l in ls]
    inv = 1.0 / (es[0] + es[1] + es[2])
    a = jnp.concatenate([os[g] * (es[g] * inv) for g in range(N_GROUPS)], axis=1)
    ya = jnp.dot((a * ga_ref[...].astype(F32)).astype(BF16), wao_ref[...], preferred_element_type=F32)

    mu = jnp.mean(cdw, axis=-1, keepdims=True)
    xc = cdw - mu
    var = jnp.mean(xc * xc, axis=-1, keepdims=True)
    yn = xc * lax.rsqrt(var + LN_EPS) * lng_ref[...] + lnb_ref[...]
    ct = _silu(yn) * gc_ref[...].astype(F32)
    yc = jnp.dot(ct.astype(BF16), wco_ref[...], preferred_element_type=F32)

    merged = ma_ref[...].astype(F32) * ya + mc_ref[...].astype(F32) * yc
    z = jnp.dot(merged.astype(BF16), wo_ref[...], preferred_element_type=F32)
    ms = jnp.mean(z * z, axis=-1, keepdims=True)
    y_ref[...] = x_ref[...] + (z * lax.rsqrt(ms + RMS_EPS)) * gpost_ref[...]


def _post_prompt_kernel(o0_ref, o1_ref, o2_ref, l0_ref, l1_ref, l2_ref, ga_ref, u_ref, up_ref,
                        gc_ref, ma_ref, mc_ref, x_ref, wao_ref, wco_ref, wo_ref, cw_ref,
                        lng_ref, lnb_ref, gpost_ref, y_ref,
                        ext_ref, shift_ref, cdw_ref, nat_ref):
    tm = u_ref.shape[0]

    def natural(ref, slot, dil):
        if dil == 1:
            return ref[...]
        rows = tm // dil
        halves = GROUP_WIDTH // LANES
        for r in range(dil):
            for half in range(halves):
                c0 = r * GROUP_WIDTH + half * LANES
                nat_ref[slot, half, pl.ds(r, rows, stride=dil), :] = ref[:, c0:c0 + LANES]
        return jnp.concatenate([nat_ref[slot, half] for half in range(halves)], axis=1)

    os = [natural(o0_ref, 0, 1), natural(o1_ref, 0, GROUPS[1][1]), natural(o2_ref, 1, GROUPS[2][1])]
    ls = [natural(l0_ref, 0, 1), natural(l1_ref, 2, GROUPS[1][1]), natural(l2_ref, 3, GROUPS[2][1])]

    ext_ref[CONV_HALO:, :] = u_ref[...]

    @pl.when(pl.program_id(1) == 0)
    def _():
        ext_ref[0:CONV_HALO, :] = jnp.zeros((CONV_HALO, CONV_WIDTH), F32)

    @pl.when(pl.program_id(1) > 0)
    def _():
        ext_ref[0:CONV_HALO, :] = up_ref[...]

    nshift = tm + CONV_HALO - SUBLANES
    for s in range(1, SUBLANES):
        shift_ref[s - 1] = ext_ref[pl.ds(s, nshift), :]

    def tap_rows(j, c0):
        off = CONV_BASE + j
        s, a = off % SUBLANES, off - off % SUBLANES
        start = pl.multiple_of(c0 + a, SUBLANES)
        return s, start

    nsub = CONV_CHUNK // SUBLANES
    for lb in range(CONV_WIDTH // LANES):
        ls_ = slice(lb * LANES, (lb + 1) * LANES)
        wts = [jnp.broadcast_to(cw_ref[j:j + 1, ls_], (SUBLANES, LANES)) for j in range(CONV_K)]

        def chunk(ci, carry, ls_=ls_, wts=wts):
            c0 = pl.multiple_of(ci * CONV_CHUNK, CONV_CHUNK)
            acc = [jnp.zeros((SUBLANES, LANES), F32) for _ in range(nsub)]
            for j in range(CONV_K):
                s, start = tap_rows(j, c0)
                if s == 0:
                    win = ext_ref[pl.ds(start, CONV_CHUNK), ls_]
                else:
                    win = shift_ref[s - 1, pl.ds(start, CONV_CHUNK), ls_]
                for k in range(nsub):
                    acc[k] = acc[k] + wts[j] * win[k * SUBLANES:(k + 1) * SUBLANES, :]
            cdw_ref[pl.ds(c0, CONV_CHUNK), ls_] = jnp.concatenate(acc, axis=0)
            return carry

        lax.fori_loop(0, tm // CONV_CHUNK, chunk, 0)

    _post_tail(os, ls, ga_ref, cdw_ref[...], gc_ref, ma_ref, mc_ref, x_ref,
               wao_ref, wco_ref, wo_ref, lng_ref, lnb_ref, gpost_ref, y_ref)


def _post_sample_kernel(o_ref, l_ref, ga_ref, u_ref, st_ref, gc_ref, ma_ref, mc_ref, x_ref,
                        wao_ref, wco_ref, wo_ref, cw_ref, lng_ref, lnb_ref, gpost_ref, y_ref):
    cdw = cw_ref[CONV_K - 1:CONV_K, :] * u_ref[...]
    for j in range(CONV_K - 1):
        cdw = cdw + cw_ref[j:j + 1, :] * st_ref[j]
    os = [o_ref[:, g * GROUP_WIDTH:(g + 1) * GROUP_WIDTH] for g in range(N_GROUPS)]
    ls = [l_ref[:, g * GROUP_WIDTH:(g + 1) * GROUP_WIDTH] for g in range(N_GROUPS)]
    _post_tail(os, ls, ga_ref, cdw, gc_ref, ma_ref, mc_ref, x_ref,
               wao_ref, wco_ref, wo_ref, lng_ref, lnb_ref, gpost_ref, y_ref)


def _weight_specs():
    full2 = (lambda *i: (0, 0))
    return [
        pl.BlockSpec((ATTN_WIDTH, D_MODEL), full2, pipeline_mode=pl.Buffered(1)),
        pl.BlockSpec((CONV_WIDTH, D_MODEL), full2, pipeline_mode=pl.Buffered(1)),
        pl.BlockSpec((D_MODEL, D_MODEL), full2, pipeline_mode=pl.Buffered(1)),
        pl.BlockSpec((CONV_K, CONV_WIDTH), full2),
        pl.BlockSpec((1, CONV_WIDTH), full2),
        pl.BlockSpec((1, CONV_WIDTH), full2),
        pl.BlockSpec((1, D_MODEL), full2),
    ]


def _post_prompt(o, l, ga, u, gc, ma, mc, x2d, weights, batch, seq):
    tm = TM_POST
    m = batch * seq
    nt = seq // tm
    row = lambda b, i: (b * nt + i, 0)
    halo = lambda b, i: (jnp.maximum((b * nt + i) * (tm // CONV_HALO) - 1, 0), 0)
    dec = [pl.BlockSpec((None, tm // dil, dil * GROUP_WIDTH), lambda b, i: (b, i, 0)) for _, dil in GROUPS]
    in_specs = (dec + dec
                + [pl.BlockSpec((tm, ATTN_WIDTH), row),
                   pl.BlockSpec((tm, CONV_WIDTH), row),
                   pl.BlockSpec((CONV_HALO, CONV_WIDTH), halo),
                   pl.BlockSpec((tm, CONV_WIDTH), row),
                   pl.BlockSpec((tm, D_MODEL), row),
                   pl.BlockSpec((tm, D_MODEL), row),
                   pl.BlockSpec((tm, D_MODEL), row)]
                + _weight_specs())
    return pl.pallas_call(
        _post_prompt_kernel,
        grid=(batch, nt),
        in_specs=in_specs,
        out_specs=pl.BlockSpec((tm, D_MODEL), row),
        out_shape=jax.ShapeDtypeStruct((m, D_MODEL), F32),
        scratch_shapes=[pltpu.VMEM((tm + CONV_HALO, CONV_WIDTH), F32),
                        pltpu.VMEM((SUBLANES - 1, tm + CONV_HALO - SUBLANES, CONV_WIDTH), F32),
                        pltpu.VMEM((tm, CONV_WIDTH), F32),
                        pltpu.VMEM((4, GROUP_WIDTH // LANES, tm, LANES), F32)],
        compiler_params=pltpu.CompilerParams(
            dimension_semantics=("parallel", "parallel"), vmem_limit_bytes=VMEM_LIMIT),
        name="post_prompt",
    )(*o, *l, ga, u, u, gc, ma, mc, x2d, *weights)


def _post_sample(o, l, ga, u, state_t, gc, ma, mc, x2d, weights, layer):
    db = x2d.shape[0]
    row = lambda i: (0, 0)
    in_specs = ([pl.BlockSpec((db, ATTN_WIDTH), row)] * 3
                + [pl.BlockSpec((db, CONV_WIDTH), row),
                   pl.BlockSpec((None, CONV_K - 1, db, CONV_WIDTH), lambda i: (layer, 0, 0, 0)),
                   pl.BlockSpec((db, CONV_WIDTH), row),
                   pl.BlockSpec((db, D_MODEL), row),
                   pl.BlockSpec((db, D_MODEL), row),
                   pl.BlockSpec((db, D_MODEL), row)]
                + _weight_specs())
    return pl.pallas_call(
        _post_sample_kernel,
        grid=(1,),
        in_specs=in_specs,
        out_specs=pl.BlockSpec((db, D_MODEL), row),
        out_shape=jax.ShapeDtypeStruct((db, D_MODEL), F32),
        compiler_params=pltpu.CompilerParams(
            dimension_semantics=("arbitrary",), vmem_limit_bytes=VMEM_LIMIT),
        name="post_sample",
    )(o, l, ga, u, state_t, gc, ma, mc, x2d, *weights)


def _rope_tables(pos):
    inv = jnp.power(jnp.float32(ROPE_THETA), -jnp.arange(0, ROT_DIM, 2, dtype=F32) / ROT_DIM)
    ang = pos.astype(F32)[:, None] * inv[None, :]
    cos, sin = jnp.cos(ang), jnp.sin(ang)
    npos = pos.shape[0]
    half = ROT_DIM // 2
    pad = jnp.zeros((npos, HEAD_DIM - ROT_DIM), F32)
    zero = jnp.zeros((npos, half), F32)
    cos_h = jnp.concatenate([cos, cos, pad + 1.0], axis=1)
    sa_h = jnp.concatenate([-sin, zero, pad], axis=1)
    sb_h = jnp.concatenate([zero, sin, pad], axis=1)
    rep = lambda t: jnp.tile(t, (1, HEADS_PER_GROUP))
    return rep(cos_h), rep(sa_h), rep(sb_h)


def _positions_minor(x):
    depth, n, npos = x.shape[:3]
    return jnp.transpose(x, (0, 1, 3, 4, 2)).reshape(depth, n, GROUP_WIDTH, npos)


def _positions_major(x):
    depth, n, _, npos = x.shape
    return jnp.transpose(x.reshape(depth, n, HEADS_PER_GROUP, HEAD_DIM, npos), (0, 1, 4, 2, 3))


def kernel(x_prompt, x_sample, cache_k0, cache_v0, cache_k1, cache_v1, cache_k2, cache_v2, state_conv,
           w_in, w_attn_out, w_conv_out, w_out, conv_w, conv_ln_g, conv_ln_b, norm_pre, norm_post):
    batch, seq, _ = x_prompt.shape
    db, dseq, _ = x_sample.shape
    depth = w_in.shape[0]
    assert dseq == 1 and seq % TM_IN == 0 and seq >= GROUPS[-1][0]
    caches = (cache_k0, cache_v0, cache_k1, cache_v1, cache_k2, cache_v2)
    for g, (win, _) in enumerate(GROUPS):
        assert caches[2 * g].shape[2] == win and caches[2 * g + 1].shape[2] == win
    caches_t = [_positions_minor(c) for c in caches]
    state_t = jnp.transpose(state_conv, (0, 2, 1, 3))

    w_in_bf = w_in.astype(BF16)
    w_ao_bf = w_attn_out.astype(BF16)
    w_co_bf = w_conv_out.astype(BF16)
    w_o_bf = w_out.astype(BF16)

    tab_p = _rope_tables(jnp.arange(seq, dtype=jnp.int32))
    tab_s = _rope_tables(jnp.full((db,), PAST_LEN, dtype=jnp.int32))

    yp = x_prompt.reshape(batch * seq, D_MODEL)
    ys = x_sample.reshape(db, D_MODEL)
    p_states, s_states = (), ()
    pc, sc = [], []
    for layer in range(depth):
        g_pre = norm_pre[layer][None, :]
        weights = (w_ao_bf[layer], w_co_bf[layer], w_o_bf[layer], conv_w[layer],
                   conv_ln_g[layer][None, :], conv_ln_b[layer][None, :], norm_post[layer][None, :])

        qkv, p_states, (ga, u, gc, ma, mc) = _inproj_prompt(
            yp, g_pre, w_in_bf[layer], tab_p, batch, seq, depth, layer, p_states)
        o, l = _attention(qkv, batch, seq)
        yp = _post_prompt(o, l, ga, u, gc, ma, mc, yp, weights, batch, seq)
        pc.append(u.reshape(batch, seq, CONV_WIDTH)[:, seq - (CONV_K - 1):])

        (q, kf, vf), (ga, u, gc, ma, mc) = _inproj_sample(ys, g_pre, w_in_bf[layer], tab_s)
        o, l, s_states = _sample_attention(q, kf, vf, caches_t, layer, s_states)
        ys = _post_sample(o, l, ga, u, state_t, gc, ma, mc, ys, weights, layer)
        sc.append(jnp.concatenate([state_conv[layer][:, 1:], u[:, None, :]], axis=1))

    pk = [_positions_major(p_states[g]) for g in range(N_GROUPS)]
    pv = [_positions_major(p_states[N_GROUPS + g]) for g in range(N_GROUPS)]
    sk = [_positions_major(s_states[2 * g]) for g in range(N_GROUPS)]
    sv = [_positions_major(s_states[2 * g + 1]) for g in range(N_GROUPS)]
    return (yp.reshape(batch, seq, D_MODEL), ys.reshape(db, 1, D_MODEL),
            pk[0], pv[0], pk[1], pv[1], pk[2], pv[2], jnp.stack(pc),
            sk[0], sv[0], sk[1], sv[1], sk[2], sv[2], jnp.stack(sc))
```

```python
import functools

import jax
import jax.numpy as jnp
from jax import lax
from jax.experimental import pallas as pl
from jax.experimental.pallas import tpu as pltpu

D_MODEL = 1024
HEAD_DIM = 64
HEADS_PER_GROUP = 4
GROUPS = ((128, 1), (512, 4), (2048, 16))
N_GROUPS = len(GROUPS)
GROUP_WIDTH = HEADS_PER_GROUP * HEAD_DIM
ATTN_WIDTH = N_GROUPS * GROUP_WIDTH
CONV_WIDTH = 768
CONV_K = 31
ROT_DIM = 16
ROPE_THETA = 500000.0
QBLK = 128
ATTN_UNROLL = 15
RMS_EPS = 1e-6
LN_EPS = 1e-5
PAST_LEN = 8192
IN_COLS = 4 * ATTN_WIDTH + 3 * CONV_WIDTH + 2 * D_MODEL

COL_Q, COL_K, COL_V, COL_GA = 0, 768, 1536, 2304
COL_CA, COL_CB, COL_GC = 3072, 3840, 4608
COL_MA, COL_MC = 5376, 6400

SUBLANES = 8
LANES = 128
CONV_HALO = 32
CONV_BASE = CONV_HALO - (CONV_K - 1)
CONV_CHUNK = 32
TM_IN = 512
TM_POST = 512
VMEM_LIMIT = 56 * 1024 * 1024

F32 = jnp.float32
BF16 = jnp.bfloat16
F32_MIN = float(jnp.finfo(jnp.float32).min)
F32_MAX = float(jnp.finfo(jnp.float32).max)


def _sigmoid(x):
    return jax.nn.sigmoid(x)


def _silu(x):
    return x * jax.nn.sigmoid(x)


def _round_bf16(x):
    return x.astype(BF16).astype(F32)


def _ln_stats(cdw):
    mu = jnp.mean(cdw, axis=-1, keepdims=True)
    xc = cdw - mu
    var = jnp.mean(xc * xc, axis=-1, keepdims=True)
    return mu, lax.rsqrt(var + LN_EPS)


def _conv_gate(cdw, mu, rstd, lng, lnb, gc_act):
    return _silu((cdw - mu) * rstd * lng + lnb) * gc_act


def _inproj_body(x_ref, g_ref, w_ref, cos_ref, sa_ref, sb_ref,
                 emit_q, emit_k, emit_v, emit_u, between, emit_gc, ga_ref, ma_ref, mc_ref):
    x = x_ref[...]
    ms = jnp.mean(x * x, axis=-1, keepdims=True)
    h = (x * lax.rsqrt(ms + RMS_EPS)) * g_ref[...]
    hb = h.astype(BF16)

    def proj(c0):
        return jnp.dot(hb, w_ref[:, c0:c0 + GROUP_WIDTH], preferred_element_type=F32)

    cos = cos_ref[...]
    sa = sa_ref[...]
    sb = sb_ref[...]

    def rope(z):
        return (z * cos + pltpu.roll(z, GROUP_WIDTH - ROT_DIM // 2, 1) * sa
                + pltpu.roll(z, ROT_DIM // 2, 1) * sb)

    for c in range(CONV_WIDTH // GROUP_WIDTH):
        a = proj(COL_CA + c * GROUP_WIDTH)
        b = proj(COL_CB + c * GROUP_WIDTH)
        emit_u(c, a * _sigmoid(b))
    for g in range(N_GROUPS):
        between()
        emit_q(g, rope(proj(COL_Q + g * GROUP_WIDTH)) * (HEAD_DIM ** -0.5))
        emit_k(g, rope(proj(COL_K + g * GROUP_WIDTH)))
        emit_v(g, proj(COL_V + g * GROUP_WIDTH))
    between()
    for c in range(CONV_WIDTH // GROUP_WIDTH):
        cs = slice(c * GROUP_WIDTH, (c + 1) * GROUP_WIDTH)
        ga_ref[:, cs] = _silu(proj(COL_GA + c * GROUP_WIDTH)).astype(BF16)
    between()
    for c in range(D_MODEL // GROUP_WIDTH):
        cs = slice(c * GROUP_WIDTH, (c + 1) * GROUP_WIDTH)
        ma_ref[:, cs] = _sigmoid(proj(COL_MA + c * GROUP_WIDTH)).astype(BF16)
    between()
    for c in range(D_MODEL // GROUP_WIDTH):
        cs = slice(c * GROUP_WIDTH, (c + 1) * GROUP_WIDTH)
        mc_ref[:, cs] = _sigmoid(proj(COL_MC + c * GROUP_WIDTH)).astype(BF16)
    between()
    for c in range(CONV_WIDTH // GROUP_WIDTH):
        emit_gc(c, _silu(proj(COL_GC + c * GROUP_WIDTH)))


N_INPROJ_IN = 9


def _inproj_prompt_kernel(*refs, n_alias, tiles_per_seq):
    x_ref, g_ref, w_ref, cos_ref, sa_ref, sb_ref, cw_ref, lng_ref, lnb_ref = refs[:N_INPROJ_IN]
    outs = refs[N_INPROJ_IN + n_alias:]
    qkv_refs = outs[0:9]
    pk_refs = outs[9:12]
    pv_refs = outs[12:15]
    ga_ref, ct_ref, ma_ref, mc_ref, utail_ref = outs[15:20]
    tmp_ref, ext_ref, shift_ref, cdw_ref = outs[20:24]
    tm = x_ref.shape[0]

    def emit_decimated(ref, z, dil):
        if dil == 1:
            ref[...] = z.astype(BF16)
            return
        rows = tm // dil
        for half in range(GROUP_WIDTH // LANES):
            tmp_ref[half] = z[:, half * LANES:(half + 1) * LANES]
        for r in range(dil):
            for half in range(GROUP_WIDTH // LANES):
                c0 = r * GROUP_WIDTH + half * LANES
                ref[:, c0:c0 + LANES] = tmp_ref[half, pl.ds(r, rows, stride=dil), :].astype(BF16)

    def emit_state(ref, z):
        keep = ref.shape[1]
        ref[...] = z[tm - keep:, :].T

    def emit_q(g, z):
        emit_decimated(qkv_refs[3 * g], z, GROUPS[g][1])

    def emit_k(g, z):
        emit_decimated(qkv_refs[3 * g + 1], z, GROUPS[g][1])
        emit_state(pk_refs[g], z)

    def emit_v(g, z):
        emit_decimated(qkv_refs[3 * g + 2], z, GROUPS[g][1])
        emit_state(pv_refs[g], z)

    first = (pl.program_id(0) % tiles_per_seq) == 0
    halo = jnp.where(first, 0.0, ext_ref[tm:tm + CONV_HALO, :])
    ext_ref[0:CONV_HALO, :] = halo

    def emit_u(c, u):
        ext_ref[CONV_HALO:, c * GROUP_WIDTH:(c + 1) * GROUP_WIDTH] = u

    def conv_lane_block(lb):
        nshift = tm + CONV_HALO - SUBLANES
        nsub = CONV_CHUNK // SUBLANES
        lanes = slice(lb * LANES, (lb + 1) * LANES)
        buf = lb % 2
        for s in range(1, SUBLANES):
            shift_ref[buf, s - 1] = ext_ref[pl.ds(s, nshift), lanes]
        wts = [jnp.broadcast_to(cw_ref[j:j + 1, lanes], (SUBLANES, LANES)) for j in range(CONV_K)]
        for c0 in range(0, tm, CONV_CHUNK):
            acc = [jnp.zeros((SUBLANES, LANES), F32) for _ in range(nsub)]
            for j in range(CONV_K):
                off = CONV_BASE + j
                s, a = off % SUBLANES, off - off % SUBLANES
                if s == 0:
                    win = ext_ref[c0 + a:c0 + a + CONV_CHUNK, lanes]
                else:
                    win = shift_ref[buf, s - 1, c0 + a:c0 + a + CONV_CHUNK, :]
                for k in range(nsub):
                    acc[k] = acc[k] + wts[j] * win[k * SUBLANES:(k + 1) * SUBLANES, :]
            cdw_ref[c0:c0 + CONV_CHUNK, lanes] = jnp.concatenate(acc, axis=0)

    pending = list(range(CONV_WIDTH // LANES))

    def between():
        if len(pending) == CONV_WIDTH // LANES:
            utail_ref[...] = ext_ref[tm:tm + CONV_HALO, :]
        if pending:
            conv_lane_block(pending.pop(0))

    def emit_gc(c, gc_act):
        cs = slice(c * GROUP_WIDTH, (c + 1) * GROUP_WIDTH)
        assert not pending
        if c == 0:
            stats.extend(_ln_stats(cdw_ref[...]))
        mu, rstd = stats
        ct_ref[:, cs] = _conv_gate(cdw_ref[:, cs], mu, rstd, lng_ref[:, cs], lnb_ref[:, cs], gc_act).astype(BF16)

    stats = []
    _inproj_body(x_ref, g_ref, w_ref, cos_ref, sa_ref, sb_ref,
                 emit_q, emit_k, emit_v, emit_u, between, emit_gc, ga_ref, ma_ref, mc_ref)


def _inproj_sample_kernel(x_ref, g_ref, w_ref, cos_ref, sa_ref, sb_ref,
                          q_ref, kf_ref, vf_ref, ga_ref, u_ref, gc_ref, ma_ref, mc_ref):
    def emit_q(g, z):
        q_ref[:, g * GROUP_WIDTH:(g + 1) * GROUP_WIDTH] = _round_bf16(z)

    def emit_k(g, z):
        kf_ref[:, g * GROUP_WIDTH:(g + 1) * GROUP_WIDTH] = z

    def emit_v(g, z):
        vf_ref[:, g * GROUP_WIDTH:(g + 1) * GROUP_WIDTH] = z

    def emit_u(c, u):
        u_ref[:, c * GROUP_WIDTH:(c + 1) * GROUP_WIDTH] = u

    def emit_gc(c, gc_act):
        gc_ref[:, c * GROUP_WIDTH:(c + 1) * GROUP_WIDTH] = gc_act.astype(BF16)

    _inproj_body(x_ref, g_ref, w_ref, cos_ref, sa_ref, sb_ref,
                 emit_q, emit_k, emit_v, emit_u, lambda: None, emit_gc, ga_ref, ma_ref, mc_ref)


def _inproj_in_specs(tm, n_tab):
    row = lambda i: (i, 0)
    full = lambda i: (0, 0)
    tab = lambda i: (i % n_tab, 0)
    return [
        pl.BlockSpec((tm, D_MODEL), row),
        pl.BlockSpec((1, D_MODEL), full),
        pl.BlockSpec((D_MODEL, IN_COLS), full, pipeline_mode=pl.Buffered(1)),
        pl.BlockSpec((tm, GROUP_WIDTH), tab),
        pl.BlockSpec((tm, GROUP_WIDTH), tab),
        pl.BlockSpec((tm, GROUP_WIDTH), tab),
    ]


def _inproj_prompt(x2d, g_pre, w_in_bf, tabs, conv_w, ln_g, ln_b, batch, seq, depth, layer, prev_states):
    tm = TM_IN
    m = batch * seq
    nt = seq // tm
    row = lambda i: (i, 0)
    full = lambda i: (0, 0)
    in_specs = _inproj_in_specs(tm, nt) + [
        pl.BlockSpec((CONV_K, CONV_WIDTH), full),
        pl.BlockSpec((1, CONV_WIDTH), full),
        pl.BlockSpec((1, CONV_WIDTH), full),
    ]
    assert len(in_specs) == N_INPROJ_IN
    n_alias = len(prev_states)
    in_specs += [pl.BlockSpec(memory_space=pl.ANY)] * n_alias

    out_shape, out_specs = [], []
    for _, dil in GROUPS:
        shp = (batch, seq // dil, dil * GROUP_WIDTH)
        spec = pl.BlockSpec((None, tm // dil, dil * GROUP_WIDTH), lambda i: (i // nt, i % nt, 0))
        out_shape += [jax.ShapeDtypeStruct(shp, BF16)] * 3
        out_specs += [spec] * 3
    for _ in range(2):
        for win, _ in GROUPS:
            keep = min(win, seq)
            blk = min(keep, tm)
            nblk = keep // blk
            imap = (lambda i, nblk=nblk:
                    (layer, i // nt, 0, jnp.maximum(i % nt - (nt - nblk), 0)))
            out_shape.append(jax.ShapeDtypeStruct((depth, batch, GROUP_WIDTH, keep), F32))
            out_specs.append(pl.BlockSpec((None, None, GROUP_WIDTH, blk), imap))
    out_shape += [jax.ShapeDtypeStruct((m, ATTN_WIDTH), BF16),
                  jax.ShapeDtypeStruct((m, CONV_WIDTH), BF16),
                  jax.ShapeDtypeStruct((m, D_MODEL), BF16),
                  jax.ShapeDtypeStruct((m, D_MODEL), BF16),
                  jax.ShapeDtypeStruct((batch, CONV_HALO, CONV_WIDTH), F32)]
    out_specs += [pl.BlockSpec((tm, ATTN_WIDTH), row),
                  pl.BlockSpec((tm, CONV_WIDTH), row),
                  pl.BlockSpec((tm, D_MODEL), row),
                  pl.BlockSpec((tm, D_MODEL), row),
                  pl.BlockSpec((None, CONV_HALO, CONV_WIDTH), lambda i: (i // nt, 0, 0))]
    aliases = {N_INPROJ_IN + j: 9 + j for j in range(n_alias)}
    outs = pl.pallas_call(
        functools.partial(_inproj_prompt_kernel, n_alias=n_alias, tiles_per_seq=nt),
        grid=(m // tm,),
        in_specs=in_specs,
        out_specs=out_specs,
        out_shape=out_shape,
        scratch_shapes=[pltpu.VMEM((GROUP_WIDTH // LANES, tm, LANES), F32),
                        pltpu.VMEM((tm + CONV_HALO, CONV_WIDTH), F32),
                        pltpu.VMEM((2, SUBLANES - 1, tm + CONV_HALO - SUBLANES, LANES), F32),
                        pltpu.VMEM((tm, CONV_WIDTH), F32)],
        input_output_aliases=aliases,
        compiler_params=pltpu.CompilerParams(
            dimension_semantics=("arbitrary",), vmem_limit_bytes=VMEM_LIMIT),
        name="inproj_prompt",
    )(x2d, g_pre, w_in_bf, *tabs, conv_w, ln_g, ln_b, *prev_states)
    return outs[0:9], outs[9:15], outs[15:19], outs[19]


def _inproj_sample(x2d, g_pre, w_in_bf, tabs):
    m = x2d.shape[0]
    row = lambda i: (i, 0)
    out_shape = ([jax.ShapeDtypeStruct((m, ATTN_WIDTH), F32)] * 3
                 + [jax.ShapeDtypeStruct((m, ATTN_WIDTH), BF16),
                    jax.ShapeDtypeStruct((m, CONV_WIDTH), F32),
                    jax.ShapeDtypeStruct((m, CONV_WIDTH), BF16),
                    jax.ShapeDtypeStruct((m, D_MODEL), BF16),
                    jax.ShapeDtypeStruct((m, D_MODEL), BF16)])
    out_specs = ([pl.BlockSpec((m, ATTN_WIDTH), row)] * 5
                 + [pl.BlockSpec((m, CONV_WIDTH), row)]
                 + [pl.BlockSpec((m, D_MODEL), row)] * 2)
    outs = pl.pallas_call(
        _inproj_sample_kernel,
        grid=(1,),
        in_specs=_inproj_in_specs(m, 1),
        out_specs=out_specs,
        out_shape=out_shape,
        compiler_params=pltpu.CompilerParams(
            dimension_semantics=("arbitrary",), vmem_limit_bytes=VMEM_LIMIT),
        name="inproj_sample",
    )(x2d, g_pre, w_in_bf, *tabs)
    return outs[0:3], outs[3:8]


def _head_masks():
    lane = lax.broadcasted_iota(jnp.int32, (1, GROUP_WIDTH), 1)
    return [lax.shift_right_logical(lane, 6) == h for h in range(HEADS_PER_GROUP)]


def _attend_block(q, k, v, cap, hmask):
    zero = jnp.zeros_like(q)
    qs = jnp.concatenate([jnp.where(hmask[h], q, zero) for h in range(HEADS_PER_GROUP)], axis=0)
    s = lax.dot_general(qs, k, (((1,), (1,)), ((), ())), preferred_element_type=F32)
    s = jnp.minimum(s, cap)
    m = jnp.max(s, axis=-1, keepdims=True)
    p = jnp.exp(s - m)
    den = jnp.sum(p, axis=-1, keepdims=True)
    r = jnp.dot(p.astype(BF16), v, preferred_element_type=F32)
    r = r / den
    lse = m + jnp.log(den)
    o = jnp.zeros((QBLK, GROUP_WIDTH), F32)
    l = jnp.zeros((QBLK, GROUP_WIDTH), F32)
    for h in range(HEADS_PER_GROUP):
        rows = slice(h * QBLK, (h + 1) * QBLK)
        o = jnp.where(hmask[h], r[rows], o)
        l = jnp.where(hmask[h], jnp.broadcast_to(lse[rows], (QBLK, GROUP_WIDTH)), l)
    return o, l


def _attn_kernel(q0_ref, k0_ref, v0_ref, q1_ref, k1_ref, v1_ref, q2_ref, k2_ref, v2_ref,
                 o0_ref, l0_ref, o1_ref, l1_ref, o2_ref, l2_ref, cap_band_ref, cap_first_ref):
    hmask = _head_masks()
    row = jnp.bitwise_and(lax.broadcasted_iota(jnp.int32, (4 * QBLK, 2 * QBLK), 0), QBLK - 1)
    col = lax.broadcasted_iota(jnp.int32, (4 * QBLK, 2 * QBLK), 1)
    cap_band_ref[...] = jnp.where((col >= row) & (col <= row + QBLK), F32_MAX, F32_MIN)
    row1 = jnp.bitwise_and(lax.broadcasted_iota(jnp.int32, (4 * QBLK, QBLK), 0), QBLK - 1)
    col1 = lax.broadcasted_iota(jnp.int32, (4 * QBLK, QBLK), 1)
    cap_first_ref[...] = jnp.where(col1 <= row1, F32_MAX, F32_MIN)

    def run_group(q_ref, k_ref, v_ref, o_ref, l_ref, dil):
        seq = q_ref.shape[0]
        nblk = seq // QBLK
        for r in range(dil):
            cs = slice(r * GROUP_WIDTH, (r + 1) * GROUP_WIDTH)
            o, l = _attend_block(q_ref[0:QBLK, cs], k_ref[0:QBLK, cs], v_ref[0:QBLK, cs],
                                 cap_first_ref[...], hmask)
            o_ref[0:QBLK, cs] = o
            l_ref[0:QBLK, cs] = l
            if nblk > 1:
                def body(j, carry):
                    qo = pl.multiple_of(j * QBLK, QBLK)
                    ko = pl.multiple_of((j - 1) * QBLK, QBLK)
                    ob, lb = _attend_block(q_ref[pl.ds(qo, QBLK), cs], k_ref[pl.ds(ko, 2 * QBLK), cs],
                                           v_ref[pl.ds(ko, 2 * QBLK), cs], cap_band_ref[...], hmask)
                    o_ref[pl.ds(qo, QBLK), cs] = ob
                    l_ref[pl.ds(qo, QBLK), cs] = lb
                    return carry
                lax.fori_loop(1, nblk, body, 0, unroll=ATTN_UNROLL)

    run_group(q0_ref, k0_ref, v0_ref, o0_ref, l0_ref, GROUPS[0][1])
    run_group(q1_ref, k1_ref, v1_ref, o1_ref, l1_ref, GROUPS[1][1])
    run_group(q2_ref, k2_ref, v2_ref, o2_ref, l2_ref, GROUPS[2][1])


def _attention(qkv, batch, seq):
    in_specs, out_shape, out_specs = [], [], []
    for _, dil in GROUPS:
        shp = (batch, seq // dil, dil * GROUP_WIDTH)
        spec = pl.BlockSpec((None, seq // dil, dil * GROUP_WIDTH), lambda b: (b, 0, 0))
        in_specs += [spec] * 3
        out_shape += [jax.ShapeDtypeStruct(shp, F32)] * 2
        out_specs += [spec, spec]
    outs = pl.pallas_call(
        _attn_kernel,
        grid=(batch,),
        in_specs=in_specs,
        out_specs=out_specs,
        out_shape=out_shape,
        scratch_shapes=[pltpu.VMEM((4 * QBLK, 2 * QBLK), F32), pltpu.VMEM((4 * QBLK, QBLK), F32)],
        compiler_params=pltpu.CompilerParams(
            dimension_semantics=("parallel",), vmem_limit_bytes=VMEM_LIMIT),
        name="dilated_attn",
    )(*qkv)
    return [outs[0], outs[2], outs[4]], [outs[1], outs[3], outs[5]]


def _sattn_kernel(*refs, n_alias):
    q_ref, kf_ref, vf_ref = refs[0:3]
    cache_refs = refs[3:9]
    outs = refs[9 + n_alias:]
    o_ref, l_ref = outs[0:2]
    state_refs = outs[2:8]

    b = pl.program_id(0)
    qrow = q_ref[pl.ds(b, 1), :]
    krow = kf_ref[pl.ds(b, 1), :]
    vrow = vf_ref[pl.ds(b, 1), :]

    ri = lax.broadcasted_iota(jnp.int32, (GROUP_WIDTH, GROUP_WIDTH), 0)
    ci = lax.broadcasted_iota(jnp.int32, (GROUP_WIDTH, GROUP_WIDTH), 1)
    diag = ri == ci
    lane_head = lax.shift_right_logical(lax.broadcasted_iota(jnp.int32, (1, GROUP_WIDTH), 1), 6)

    def to_col(row):
        return jnp.sum(jnp.where(diag, jnp.broadcast_to(row, (GROUP_WIDTH, GROUP_WIDTH)), 0.0),
                       axis=1, keepdims=True)

    def to_row(col):
        return jnp.sum(jnp.where(diag, jnp.broadcast_to(col, (GROUP_WIDTH, GROUP_WIDTH)), 0.0),
                       axis=0, keepdims=True)

    def head_sum(x):
        return jnp.sum(x.reshape(HEADS_PER_GROUP, HEAD_DIM, x.shape[1]), axis=1)

    def head_row(x):
        out = jnp.zeros((1, GROUP_WIDTH), F32)
        for h in range(HEADS_PER_GROUP):
            out = jnp.where(lane_head == h, jnp.broadcast_to(x[h:h + 1, :], (1, GROUP_WIDTH)), out)
        return out

    for g, (win, dil) in enumerate(GROUPS):
        cs = slice(g * GROUP_WIDTH, (g + 1) * GROUP_WIDTH)
        ck_ref, cv_ref = cache_refs[2 * g], cache_refs[2 * g + 1]
        sk_ref, sv_ref = state_refs[2 * g], state_refs[2 * g + 1]
        length = ck_ref.shape[1]
        qc = to_col(qrow[:, cs])
        kc = to_col(krow[:, cs])
        vc = to_col(vrow[:, cs])

        kmat = ck_ref[...]
        s = head_sum(_round_bf16(kmat) * qc)
        lane = lax.broadcasted_iota(jnp.int32, (HEADS_PER_GROUP, length), 1)
        s = jnp.where(jnp.bitwise_and(lane, dil - 1) == 0, s, F32_MIN)
        s_new = head_sum(jnp.broadcast_to(_round_bf16(kc) * qc, (GROUP_WIDTH, LANES)))[:, 0:1]
        m = jnp.maximum(jnp.max(s, axis=1, keepdims=True), s_new)
        p = jnp.exp(s - m)
        p_new = jnp.exp(s_new - m)
        den = jnp.sum(p, axis=1, keepdims=True) + p_new

        vmat = cv_ref[...]
        pb = _round_bf16(p)
        p_full = jnp.broadcast_to(pb[:, None, :], (HEADS_PER_GROUP, HEAD_DIM, length)).reshape(GROUP_WIDTH, length)
        acc_c = jnp.sum(p_full * _round_bf16(vmat), axis=1, keepdims=True)
        acc_row = to_row(acc_c) + head_row(_round_bf16(p_new)) * _round_bf16(vrow[:, cs])
        den_row = head_row(den)
        o_ref[pl.ds(b, 1), cs] = acc_row / den_row
        l_ref[pl.ds(b, 1), cs] = head_row(m) + jnp.log(den_row)

        last = lax.broadcasted_iota(jnp.int32, (GROUP_WIDTH, length), 1) == length - 1
        sk_ref[...] = jnp.where(last, kc, pltpu.roll(kmat, length - 1, 1))
        sv_ref[...] = jnp.where(last, vc, pltpu.roll(vmat, length - 1, 1))


def _sample_attention(q, kf, vf, caches_t, layer, prev_states):
    db = q.shape[0]
    depth = caches_t[0].shape[0]
    n_alias = len(prev_states)
    full = pl.BlockSpec((db, ATTN_WIDTH), lambda b: (0, 0))
    in_specs = [full] * 3
    out_shape = [jax.ShapeDtypeStruct((db, ATTN_WIDTH), F32)] * 2
    out_specs = [full, full]
    for c in caches_t:
        length = c.shape[3]
        spec = pl.BlockSpec((None, None, GROUP_WIDTH, length), lambda b: (layer, b, 0, 0))
        in_specs.append(spec)
        out_shape.append(jax.ShapeDtypeStruct((depth, db, GROUP_WIDTH, length), F32))
        out_specs.append(spec)
    in_specs += [pl.BlockSpec(memory_space=pl.ANY)] * n_alias
    aliases = {9 + j: 2 + j for j in range(n_alias)}
    outs = pl.pallas_call(
        functools.partial(_sattn_kernel, n_alias=n_alias),
        grid=(db,),
        in_specs=in_specs,
        out_specs=out_specs,
        out_shape=out_shape,
        input_output_aliases=aliases,
        compiler_params=pltpu.CompilerParams(
            dimension_semantics=("arbitrary",), vmem_limit_bytes=VMEM_LIMIT),
        name="sample_attn",
    )(q, kf, vf, *caches_t, *prev_states)
    return outs[0], outs[1], outs[2:8]


def _post_tail(os, ls, ga_ref, ct, ma_ref, mc_ref, x_ref, wao_ref, wco_ref, wo_ref, gpost_ref, y_ref):
    mx = jnp.maximum(jnp.maximum(ls[0], ls[1]), ls[2])
    es = [jnp.exp(l - mx) for l in ls]
    inv = 1.0 / (es[0] + es[1] + es[2])
    a = jnp.concatenate([os[g] * (es[g] * inv) for g in range(N_GROUPS)], axis=1)
    ya = jnp.dot((a * ga_ref[...].astype(F32)).astype(BF16), wao_ref[...], preferred_element_type=F32)
    yc = jnp.dot(ct, wco_ref[...], preferred_element_type=F32)
    merged = ma_ref[...].astype(F32) * ya + mc_ref[...].astype(F32) * yc
    z = jnp.dot(merged.astype(BF16), wo_ref[...], preferred_element_type=F32)
    ms = jnp.mean(z * z, axis=-1, keepdims=True)
    y_ref[...] = x_ref[...] + (z * lax.rsqrt(ms + RMS_EPS)) * gpost_ref[...]


def _post_prompt_kernel(o0_ref, o1_ref, o2_ref, l0_ref, l1_ref, l2_ref, ga_ref, ct_ref, ma_ref, mc_ref,
                        x_ref, wao_ref, wco_ref, wo_ref, gpost_ref, y_ref, nat_ref):
    tm = x_ref.shape[0]

    def natural(ref, slot, dil):
        if dil == 1:
            return ref[...]
        rows = tm // dil
        halves = GROUP_WIDTH // LANES
        for r in range(dil):
            for half in range(halves):
                c0 = r * GROUP_WIDTH + half * LANES
                nat_ref[slot, half, pl.ds(r, rows, stride=dil), :] = ref[:, c0:c0 + LANES]
        return jnp.concatenate([nat_ref[slot, half] for half in range(halves)], axis=1)

    os = [natural(o0_ref, 0, 1), natural(o1_ref, 0, GROUPS[1][1]), natural(o2_ref, 1, GROUPS[2][1])]
    ls = [natural(l0_ref, 0, 1), natural(l1_ref, 2, GROUPS[1][1]), natural(l2_ref, 3, GROUPS[2][1])]
    _post_tail(os, ls, ga_ref, ct_ref[...], ma_ref, mc_ref, x_ref, wao_ref, wco_ref, wo_ref, gpost_ref, y_ref)


def _post_sample_kernel(o_ref, l_ref, ga_ref, u_ref, st_ref, gc_ref, ma_ref, mc_ref, x_ref,
                        wao_ref, wco_ref, wo_ref, gpost_ref, cw_ref, lng_ref, lnb_ref, y_ref):
    cdw = cw_ref[CONV_K - 1:CONV_K, :] * u_ref[...]
    for j in range(CONV_K - 1):
        cdw = cdw + cw_ref[j:j + 1, :] * st_ref[j]
    mu, rstd = _ln_stats(cdw)
    ct = _conv_gate(cdw, mu, rstd, lng_ref[...], lnb_ref[...], gc_ref[...].astype(F32)).astype(BF16)
    os = [o_ref[:, g * GROUP_WIDTH:(g + 1) * GROUP_WIDTH] for g in range(N_GROUPS)]
    ls = [l_ref[:, g * GROUP_WIDTH:(g + 1) * GROUP_WIDTH] for g in range(N_GROUPS)]
    _post_tail(os, ls, ga_ref, ct, ma_ref, mc_ref, x_ref, wao_ref, wco_ref, wo_ref, gpost_ref, y_ref)


def _weight_specs():
    full2 = (lambda *i: (0, 0))
    return [
        pl.BlockSpec((ATTN_WIDTH, D_MODEL), full2, pipeline_mode=pl.Buffered(1)),
        pl.BlockSpec((CONV_WIDTH, D_MODEL), full2, pipeline_mode=pl.Buffered(1)),
        pl.BlockSpec((D_MODEL, D_MODEL), full2, pipeline_mode=pl.Buffered(1)),
        pl.BlockSpec((1, D_MODEL), full2),
    ]


def _post_prompt(o, l, ga, ct, ma, mc, x2d, weights, batch, seq):
    tm = TM_POST
    m = batch * seq
    nt = seq // tm
    row = lambda b, i: (b * nt + i, 0)
    dec = [pl.BlockSpec((None, tm // dil, dil * GROUP_WIDTH), lambda b, i: (b, i, 0)) for _, dil in GROUPS]
    in_specs = (dec + dec
                + [pl.BlockSpec((tm, ATTN_WIDTH), row),
                   pl.BlockSpec((tm, CONV_WIDTH), row),
                   pl.BlockSpec((tm, D_MODEL), row),
                   pl.BlockSpec((tm, D_MODEL), row),
                   pl.BlockSpec((tm, D_MODEL), row)]
                + _weight_specs())
    return pl.pallas_call(
        _post_prompt_kernel,
        grid=(batch, nt),
        in_specs=in_specs,
        out_specs=pl.BlockSpec((tm, D_MODEL), row),
        out_shape=jax.ShapeDtypeStruct((m, D_MODEL), F32),
        scratch_shapes=[pltpu.VMEM((4, GROUP_WIDTH // LANES, tm, LANES), F32)],
        compiler_params=pltpu.CompilerParams(
            dimension_semantics=("parallel", "parallel"), vmem_limit_bytes=VMEM_LIMIT),
        name="post_prompt",
    )(*o, *l, ga, ct, ma, mc, x2d, *weights)


def _post_sample(o, l, ga, u, state_t, gc, ma, mc, x2d, weights, conv_w, ln_g, ln_b, layer):
    db = x2d.shape[0]
    row = lambda i: (0, 0)
    in_specs = ([pl.BlockSpec((db, ATTN_WIDTH), row)] * 3
                + [pl.BlockSpec((db, CONV_WIDTH), row),
                   pl.BlockSpec((None, CONV_K - 1, db, CONV_WIDTH), lambda i: (layer, 0, 0, 0)),
                   pl.BlockSpec((db, CONV_WIDTH), row),
                   pl.BlockSpec((db, D_MODEL), row),
                   pl.BlockSpec((db, D_MODEL), row),
                   pl.BlockSpec((db, D_MODEL), row)]
                + _weight_specs()
                + [pl.BlockSpec((CONV_K, CONV_WIDTH), row),
                   pl.BlockSpec((1, CONV_WIDTH), row),
                   pl.BlockSpec((1, CONV_WIDTH), row)])
    return pl.pallas_call(
        _post_sample_kernel,
        grid=(1,),
        in_specs=in_specs,
        out_specs=pl.BlockSpec((db, D_MODEL), row),
        out_shape=jax.ShapeDtypeStruct((db, D_MODEL), F32),
        compiler_params=pltpu.CompilerParams(
            dimension_semantics=("arbitrary",), vmem_limit_bytes=VMEM_LIMIT),
        name="post_sample",
    )(o, l, ga, u, state_t, gc, ma, mc, x2d, *weights, conv_w, ln_g, ln_b)


def _rope_tables(pos):
    inv = jnp.power(jnp.float32(ROPE_THETA), -jnp.arange(0, ROT_DIM, 2, dtype=F32) / ROT_DIM)
    ang = pos.astype(F32)[:, None] * inv[None, :]
    cos, sin = jnp.cos(ang), jnp.sin(ang)
    npos = pos.shape[0]
    half = ROT_DIM // 2
    pad = jnp.zeros((npos, HEAD_DIM - ROT_DIM), F32)
    zero = jnp.zeros((npos, half), F32)
    cos_h = jnp.concatenate([cos, cos, pad + 1.0], axis=1)
    sa_h = jnp.concatenate([-sin, zero, pad], axis=1)
    sb_h = jnp.concatenate([zero, sin, pad], axis=1)
    rep = lambda t: jnp.tile(t, (1, HEADS_PER_GROUP))
    return rep(cos_h), rep(sa_h), rep(sb_h)


def _positions_minor(x):
    depth, n, npos = x.shape[:3]
    return jnp.transpose(x, (0, 1, 3, 4, 2)).reshape(depth, n, GROUP_WIDTH, npos)


def _positions_major(x):
    depth, n, _, npos = x.shape
    return jnp.transpose(x.reshape(depth, n, HEADS_PER_GROUP, HEAD_DIM, npos), (0, 1, 4, 2, 3))


def kernel(x_prompt, x_sample, cache_k0, cache_v0, cache_k1, cache_v1, cache_k2, cache_v2, state_conv,
           w_in, w_attn_out, w_conv_out, w_out, conv_w, conv_ln_g, conv_ln_b, norm_pre, norm_post):
    batch, seq, _ = x_prompt.shape
    db, dseq, _ = x_sample.shape
    depth = w_in.shape[0]
    assert dseq == 1 and seq % TM_IN == 0 and seq >= GROUPS[-1][0]
    caches = (cache_k0, cache_v0, cache_k1, cache_v1, cache_k2, cache_v2)
    for g, (win, _) in enumerate(GROUPS):
        assert caches[2 * g].shape[2] == win and caches[2 * g + 1].shape[2] == win
    caches_t = [_positions_minor(c) for c in caches]
    state_t = jnp.transpose(state_conv, (0, 2, 1, 3))

    w_in_bf = w_in.astype(BF16)
    w_ao_bf = w_attn_out.astype(BF16)
    w_co_bf = w_conv_out.astype(BF16)
    w_o_bf = w_out.astype(BF16)

    tab_p = _rope_tables(jnp.arange(seq, dtype=jnp.int32))
    tab_s = _rope_tables(jnp.full((db,), PAST_LEN, dtype=jnp.int32))

    yp = x_prompt.reshape(batch * seq, D_MODEL)
    ys = x_sample.reshape(db, D_MODEL)
    p_states, s_states = (), ()
    pc, sc = [], []
    for layer in range(depth):
        g_pre = norm_pre[layer][None, :]
        cw = conv_w[layer]
        ln_g = conv_ln_g[layer][None, :]
        ln_b = conv_ln_b[layer][None, :]
        weights = (w_ao_bf[layer], w_co_bf[layer], w_o_bf[layer], norm_post[layer][None, :])

        qkv, p_states, (ga, ct, ma, mc), utail = _inproj_prompt(
            yp, g_pre, w_in_bf[layer], tab_p, cw, ln_g, ln_b, batch, seq, depth, layer, p_states)
        o, l = _attention(qkv, batch, seq)
        yp = _post_prompt(o, l, ga, ct, ma, mc, yp, weights, batch, seq)
        pc.append(utail[:, CONV_HALO - (CONV_K - 1):])

        (q, kf, vf), (ga, u, gc, ma, mc) = _inproj_sample(ys, g_pre, w_in_bf[layer], tab_s)
        o, l, s_states = _sample_attention(q, kf, vf, caches_t, layer, s_states)
        ys = _post_sample(o, l, ga, u, state_t, gc, ma, mc, ys, weights, cw, ln_g, ln_b, layer)
        sc.append(jnp.concatenate([state_conv[layer][:, 1:], u[:, None, :]], axis=1))

    pk = [_positions_major(p_states[g]) for g in range(N_GROUPS)]
    pv = [_positions_major(p_states[N_GROUPS + g]) for g in range(N_GROUPS)]
    sk = [_positions_major(s_states[2 * g]) for g in range(N_GROUPS)]
    sv = [_positions_major(s_states[2 * g + 1]) for g in range(N_GROUPS)]
    return (yp.reshape(batch, seq, D_MODEL), ys.reshape(db, 1, D_MODEL),
            pk[0], pv[0], pk[1], pv[1], pk[2], pv[2], jnp.stack(pc),
            sk[0], sv[0], sk[1], sv[1], sk[2], sv[2], jnp.stack(sc))
```

```python
import functools

import jax
import jax.numpy as jnp
from jax import lax
from jax.experimental import pallas as pl
from jax.experimental.pallas import tpu as pltpu

D_MODEL = 1024
HEAD_DIM = 64
HEADS_PER_GROUP = 4
GROUPS = ((128, 1), (512, 4), (2048, 16))
N_GROUPS = len(GROUPS)
GROUP_WIDTH = HEADS_PER_GROUP * HEAD_DIM
ATTN_WIDTH = N_GROUPS * GROUP_WIDTH
CONV_WIDTH = 768
CONV_K = 31
ROT_DIM = 16
ROPE_THETA = 500000.0
QBLK = 128
ATTN_UNROLL = 15
RMS_EPS = 1e-6
LN_EPS = 1e-5
PAST_LEN = 8192
IN_COLS = 4 * ATTN_WIDTH + 3 * CONV_WIDTH + 2 * D_MODEL

COL_Q, COL_K, COL_V, COL_GA = 0, 768, 1536, 2304
COL_CA, COL_CB, COL_GC = 3072, 3840, 4608
COL_MA, COL_MC = 5376, 6400

SUBLANES = 8
LANES = 128
CONV_HALO = 32
CONV_BASE = CONV_HALO - (CONV_K - 1)
CONV_CHUNK = 64
TM_IN = 512
TM_POST = 512
VMEM_LIMIT = 56 * 1024 * 1024

F32 = jnp.float32
BF16 = jnp.bfloat16
F32_MIN = float(jnp.finfo(jnp.float32).min)
F32_MAX = float(jnp.finfo(jnp.float32).max)


def _sigmoid(x):
    return jax.nn.sigmoid(x)


def _silu(x):
    return x * jax.nn.sigmoid(x)


def _round_bf16(x):
    return x.astype(BF16).astype(F32)


def _ln_stats(cdw):
    mu = jnp.mean(cdw, axis=-1, keepdims=True)
    xc = cdw - mu
    var = jnp.mean(xc * xc, axis=-1, keepdims=True)
    return mu, lax.rsqrt(var + LN_EPS)


def _conv_gate(cdw, mu, rstd, lng, lnb, gc_act):
    return _silu((cdw - mu) * rstd * lng + lnb) * gc_act


def _inproj_body(x_ref, g_ref, w_ref, cos_ref, sa_ref, sb_ref,
                 emit_q, emit_k, emit_v, emit_u, between, emit_gc, ga_ref, ma_ref, mc_ref):
    x = x_ref[...]
    ms = jnp.mean(x * x, axis=-1, keepdims=True)
    h = (x * lax.rsqrt(ms + RMS_EPS)) * g_ref[...]
    hb = h.astype(BF16)

    def proj(c0):
        return jnp.dot(hb, w_ref[:, c0:c0 + GROUP_WIDTH], preferred_element_type=F32)

    cos = cos_ref[...]
    sa = sa_ref[...]
    sb = sb_ref[...]

    def rope(z):
        return (z * cos + pltpu.roll(z, GROUP_WIDTH - ROT_DIM // 2, 1) * sa
                + pltpu.roll(z, ROT_DIM // 2, 1) * sb)

    for c in range(CONV_WIDTH // GROUP_WIDTH):
        a = proj(COL_CA + c * GROUP_WIDTH)
        b = proj(COL_CB + c * GROUP_WIDTH)
        emit_u(c, a * _sigmoid(b))
    for g in range(N_GROUPS):
        between()
        emit_q(g, rope(proj(COL_Q + g * GROUP_WIDTH)) * (HEAD_DIM ** -0.5))
        emit_k(g, rope(proj(COL_K + g * GROUP_WIDTH)))
        emit_v(g, proj(COL_V + g * GROUP_WIDTH))
    between()
    for c in range(CONV_WIDTH // GROUP_WIDTH):
        cs = slice(c * GROUP_WIDTH, (c + 1) * GROUP_WIDTH)
        ga_ref[:, cs] = _silu(proj(COL_GA + c * GROUP_WIDTH)).astype(BF16)
    between()
    for c in range(D_MODEL // GROUP_WIDTH):
        cs = slice(c * GROUP_WIDTH, (c + 1) * GROUP_WIDTH)
        ma_ref[:, cs] = _sigmoid(proj(COL_MA + c * GROUP_WIDTH)).astype(BF16)
    between()
    for c in range(D_MODEL // GROUP_WIDTH):
        cs = slice(c * GROUP_WIDTH, (c + 1) * GROUP_WIDTH)
        mc_ref[:, cs] = _sigmoid(proj(COL_MC + c * GROUP_WIDTH)).astype(BF16)
    between()
    for c in range(CONV_WIDTH // GROUP_WIDTH):
        emit_gc(c, _silu(proj(COL_GC + c * GROUP_WIDTH)))


N_INPROJ_IN = 9


def _inproj_prompt_kernel(*refs, n_alias, tiles_per_seq):
    x_ref, g_ref, w_ref, cos_ref, sa_ref, sb_ref, cw_ref, lng_ref, lnb_ref = refs[:N_INPROJ_IN]
    outs = refs[N_INPROJ_IN + n_alias:]
    qkv_refs = outs[0:9]
    pk_refs = outs[9:12]
    pv_refs = outs[12:15]
    ga_ref, ct_ref, ma_ref, mc_ref, utail_ref = outs[15:20]
    tmp_ref, ext_ref, shift_ref, cdw_ref = outs[20:24]
    tm = x_ref.shape[0]

    def emit_decimated(ref, z, dil):
        if dil == 1:
            ref[...] = z.astype(BF16)
            return
        rows = tm // dil
        for half in range(GROUP_WIDTH // LANES):
            tmp_ref[half] = z[:, half * LANES:(half + 1) * LANES]
        for r in range(dil):
            for half in range(GROUP_WIDTH // LANES):
                c0 = r * GROUP_WIDTH + half * LANES
                ref[:, c0:c0 + LANES] = tmp_ref[half, pl.ds(r, rows, stride=dil), :].astype(BF16)

    def emit_state(ref, z):
        keep = ref.shape[1]
        ref[...] = z[tm - keep:, :].T

    def emit_q(g, z):
        emit_decimated(qkv_refs[3 * g], z, GROUPS[g][1])

    def emit_k(g, z):
        emit_decimated(qkv_refs[3 * g + 1], z, GROUPS[g][1])
        emit_state(pk_refs[g], z)

    def emit_v(g, z):
        emit_decimated(qkv_refs[3 * g + 2], z, GROUPS[g][1])
        emit_state(pv_refs[g], z)

    first = (pl.program_id(0) % tiles_per_seq) == 0
    halo = jnp.where(first, 0.0, ext_ref[tm:tm + CONV_HALO, :])
    ext_ref[0:CONV_HALO, :] = halo

    def emit_u(c, u):
        ext_ref[CONV_HALO:, c * GROUP_WIDTH:(c + 1) * GROUP_WIDTH] = u

    def conv_lane_block(lb):
        nshift = tm + CONV_HALO - SUBLANES
        nsub = CONV_CHUNK // SUBLANES
        lanes = slice(lb * LANES, (lb + 1) * LANES)
        buf = lb % 2
        for s in range(1, SUBLANES):
            shift_ref[buf, s - 1] = ext_ref[pl.ds(s, nshift), lanes]
        wts = [jnp.broadcast_to(cw_ref[j:j + 1, lanes], (SUBLANES, LANES)) for j in range(CONV_K)]
        for c0 in range(0, tm, CONV_CHUNK):
            acc = [jnp.zeros((SUBLANES, LANES), F32) for _ in range(nsub)]
            for j in range(CONV_K):
                off = CONV_BASE + j
                s, a = off % SUBLANES, off - off % SUBLANES
                if s == 0:
                    win = ext_ref[c0 + a:c0 + a + CONV_CHUNK, lanes]
                else:
                    win = shift_ref[buf, s - 1, c0 + a:c0 + a + CONV_CHUNK, :]
                for k in range(nsub):
                    acc[k] = acc[k] + wts[j] * win[k * SUBLANES:(k + 1) * SUBLANES, :]
            cdw_ref[c0:c0 + CONV_CHUNK, lanes] = jnp.concatenate(acc, axis=0)

    pending = list(range(CONV_WIDTH // LANES))

    def between():
        if len(pending) == CONV_WIDTH // LANES:
            utail_ref[...] = ext_ref[tm:tm + CONV_HALO, :]
        if pending:
            conv_lane_block(pending.pop(0))

    def emit_gc(c, gc_act):
        cs = slice(c * GROUP_WIDTH, (c + 1) * GROUP_WIDTH)
        assert not pending
        if c == 0:
            stats.extend(_ln_stats(cdw_ref[...]))
        mu, rstd = stats
        ct_ref[:, cs] = _conv_gate(cdw_ref[:, cs], mu, rstd, lng_ref[:, cs], lnb_ref[:, cs], gc_act).astype(BF16)

    stats = []
    _inproj_body(x_ref, g_ref, w_ref, cos_ref, sa_ref, sb_ref,
                 emit_q, emit_k, emit_v, emit_u, between, emit_gc, ga_ref, ma_ref, mc_ref)


def _inproj_sample_kernel(x_ref, g_ref, w_ref, cos_ref, sa_ref, sb_ref,
                          q_ref, kf_ref, vf_ref, ga_ref, u_ref, gc_ref, ma_ref, mc_ref):
    def emit_q(g, z):
        q_ref[:, g * GROUP_WIDTH:(g + 1) * GROUP_WIDTH] = _round_bf16(z)

    def emit_k(g, z):
        kf_ref[:, g * GROUP_WIDTH:(g + 1) * GROUP_WIDTH] = z

    def emit_v(g, z):
        vf_ref[:, g * GROUP_WIDTH:(g + 1) * GROUP_WIDTH] = z

    def emit_u(c, u):
        u_ref[:, c * GROUP_WIDTH:(c + 1) * GROUP_WIDTH] = u

    def emit_gc(c, gc_act):
        gc_ref[:, c * GROUP_WIDTH:(c + 1) * GROUP_WIDTH] = gc_act.astype(BF16)

    _inproj_body(x_ref, g_ref, w_ref, cos_ref, sa_ref, sb_ref,
                 emit_q, emit_k, emit_v, emit_u, lambda: None, emit_gc, ga_ref, ma_ref, mc_ref)


def _inproj_in_specs(tm, n_tab):
    row = lambda i: (i, 0)
    full = lambda i: (0, 0)
    tab = lambda i: (i % n_tab, 0)
    return [
        pl.BlockSpec((tm, D_MODEL), row),
        pl.BlockSpec((1, D_MODEL), full),
        pl.BlockSpec((D_MODEL, IN_COLS), full, pipeline_mode=pl.Buffered(1)),
        pl.BlockSpec((tm, GROUP_WIDTH), tab),
        pl.BlockSpec((tm, GROUP_WIDTH), tab),
        pl.BlockSpec((tm, GROUP_WIDTH), tab),
    ]


def _inproj_prompt(x2d, g_pre, w_in_bf, tabs, conv_w, ln_g, ln_b, batch, seq, depth, layer, prev_states):
    tm = TM_IN
    m = batch * seq
    nt = seq // tm
    row = lambda i: (i, 0)
    full = lambda i: (0, 0)
    in_specs = _inproj_in_specs(tm, nt) + [
        pl.BlockSpec((CONV_K, CONV_WIDTH), full),
        pl.BlockSpec((1, CONV_WIDTH), full),
        pl.BlockSpec((1, CONV_WIDTH), full),
    ]
    assert len(in_specs) == N_INPROJ_IN
    n_alias = len(prev_states)
    in_specs += [pl.BlockSpec(memory_space=pl.ANY)] * n_alias

    out_shape, out_specs = [], []
    for _, dil in GROUPS:
        shp = (batch, seq // dil, dil * GROUP_WIDTH)
        spec = pl.BlockSpec((None, tm // dil, dil * GROUP_WIDTH), lambda i: (i // nt, i % nt, 0))
        out_shape += [jax.ShapeDtypeStruct(shp, BF16)] * 3
        out_specs += [spec] * 3
    for _ in range(2):
        for win, _ in GROUPS:
            keep = min(win, seq)
            blk = min(keep, tm)
            nblk = keep // blk
            imap = (lambda i, nblk=nblk:
                    (layer, i // nt, 0, jnp.maximum(i % nt - (nt - nblk), 0)))
            out_shape.append(jax.ShapeDtypeStruct((depth, batch, GROUP_WIDTH, keep), F32))
            out_specs.append(pl.BlockSpec((None, None, GROUP_WIDTH, blk), imap))
    out_shape += [jax.ShapeDtypeStruct((m, ATTN_WIDTH), BF16),
                  jax.ShapeDtypeStruct((m, CONV_WIDTH), BF16),
                  jax.ShapeDtypeStruct((m, D_MODEL), BF16),
                  jax.ShapeDtypeStruct((m, D_MODEL), BF16),
                  jax.ShapeDtypeStruct((batch, CONV_HALO, CONV_WIDTH), F32)]
    out_specs += [pl.BlockSpec((tm, ATTN_WIDTH), row),
                  pl.BlockSpec((tm, CONV_WIDTH), row),
                  pl.BlockSpec((tm, D_MODEL), row),
                  pl.BlockSpec((tm, D_MODEL), row),
                  pl.BlockSpec((None, CONV_HALO, CONV_WIDTH), lambda i: (i // nt, 0, 0))]
    aliases = {N_INPROJ_IN + j: 9 + j for j in range(n_alias)}
    outs = pl.pallas_call(
        functools.partial(_inproj_prompt_kernel, n_alias=n_alias, tiles_per_seq=nt),
        grid=(m // tm,),
        in_specs=in_specs,
        out_specs=out_specs,
        out_shape=out_shape,
        scratch_shapes=[pltpu.VMEM((GROUP_WIDTH // LANES, tm, LANES), F32),
                        pltpu.VMEM((tm + CONV_HALO, CONV_WIDTH), F32),
                        pltpu.VMEM((2, SUBLANES - 1, tm + CONV_HALO - SUBLANES, LANES), F32),
                        pltpu.VMEM((tm, CONV_WIDTH), F32)],
        input_output_aliases=aliases,
        compiler_params=pltpu.CompilerParams(
            dimension_semantics=("arbitrary",), vmem_limit_bytes=VMEM_LIMIT),
        name="inproj_prompt",
    )(x2d, g_pre, w_in_bf, *tabs, conv_w, ln_g, ln_b, *prev_states)
    return outs[0:9], outs[9:15], outs[15:19], outs[19]


def _inproj_sample(x2d, g_pre, w_in_bf, tabs):
    m = x2d.shape[0]
    row = lambda i: (i, 0)
    out_shape = ([jax.ShapeDtypeStruct((m, ATTN_WIDTH), F32)] * 3
                 + [jax.ShapeDtypeStruct((m, ATTN_WIDTH), BF16),
                    jax.ShapeDtypeStruct((m, CONV_WIDTH), F32),
                    jax.ShapeDtypeStruct((m, CONV_WIDTH), BF16),
                    jax.ShapeDtypeStruct((m, D_MODEL), BF16),
                    jax.ShapeDtypeStruct((m, D_MODEL), BF16)])
    out_specs = ([pl.BlockSpec((m, ATTN_WIDTH), row)] * 5
                 + [pl.BlockSpec((m, CONV_WIDTH), row)]
                 + [pl.BlockSpec((m, D_MODEL), row)] * 2)
    outs = pl.pallas_call(
        _inproj_sample_kernel,
        grid=(1,),
        in_specs=_inproj_in_specs(m, 1),
        out_specs=out_specs,
        out_shape=out_shape,
        compiler_params=pltpu.CompilerParams(
            dimension_semantics=("arbitrary",), vmem_limit_bytes=VMEM_LIMIT),
        name="inproj_sample",
    )(x2d, g_pre, w_in_bf, *tabs)
    return outs[0:3], outs[3:8]


def _head_masks():
    lane = lax.broadcasted_iota(jnp.int32, (1, GROUP_WIDTH), 1)
    return [lax.shift_right_logical(lane, 6) == h for h in range(HEADS_PER_GROUP)]


def _attend_block(q, k, v, cap, hmask):
    zero = jnp.zeros_like(q)
    qs = jnp.concatenate([jnp.where(hmask[h], q, zero) for h in range(HEADS_PER_GROUP)], axis=0)
    s = lax.dot_general(qs, k, (((1,), (1,)), ((), ())), preferred_element_type=F32)
    s = jnp.minimum(s, cap)
    m = jnp.max(s, axis=-1, keepdims=True)
    p = jnp.exp(s - m)
    den = jnp.sum(p, axis=-1, keepdims=True)
    r = jnp.dot(p.astype(BF16), v, preferred_element_type=F32)
    r = r / den
    lse = m + jnp.log(den)
    o = jnp.zeros((QBLK, GROUP_WIDTH), F32)
    l = jnp.zeros((QBLK, GROUP_WIDTH), F32)
    for h in range(HEADS_PER_GROUP):
        rows = slice(h * QBLK, (h + 1) * QBLK)
        o = jnp.where(hmask[h], r[rows], o)
        l = jnp.where(hmask[h], jnp.broadcast_to(lse[rows], (QBLK, GROUP_WIDTH)), l)
    return o, l


def _attn_kernel(q0_ref, k0_ref, v0_ref, q1_ref, k1_ref, v1_ref, q2_ref, k2_ref, v2_ref,
                 o0_ref, l0_ref, o1_ref, l1_ref, o2_ref, l2_ref, cap_band_ref, cap_first_ref):
    hmask = _head_masks()
    row = jnp.bitwise_and(lax.broadcasted_iota(jnp.int32, (4 * QBLK, 2 * QBLK), 0), QBLK - 1)
    col = lax.broadcasted_iota(jnp.int32, (4 * QBLK, 2 * QBLK), 1)
    cap_band_ref[...] = jnp.where((col >= row) & (col <= row + QBLK), F32_MAX, F32_MIN)
    row1 = jnp.bitwise_and(lax.broadcasted_iota(jnp.int32, (4 * QBLK, QBLK), 0), QBLK - 1)
    col1 = lax.broadcasted_iota(jnp.int32, (4 * QBLK, QBLK), 1)
    cap_first_ref[...] = jnp.where(col1 <= row1, F32_MAX, F32_MIN)

    def run_group(q_ref, k_ref, v_ref, o_ref, l_ref, dil):
        seq = q_ref.shape[0]
        nblk = seq // QBLK
        for r in range(dil):
            cs = slice(r * GROUP_WIDTH, (r + 1) * GROUP_WIDTH)
            o, l = _attend_block(q_ref[0:QBLK, cs], k_ref[0:QBLK, cs], v_ref[0:QBLK, cs],
                                 cap_first_ref[...], hmask)
            o_ref[0:QBLK, cs] = o
            l_ref[0:QBLK, cs] = l
            if nblk > 1:
                def body(j, carry):
                    qo = pl.multiple_of(j * QBLK, QBLK)
                    ko = pl.multiple_of((j - 1) * QBLK, QBLK)
                    ob, lb = _attend_block(q_ref[pl.ds(qo, QBLK), cs], k_ref[pl.ds(ko, 2 * QBLK), cs],
                                           v_ref[pl.ds(ko, 2 * QBLK), cs], cap_band_ref[...], hmask)
                    o_ref[pl.ds(qo, QBLK), cs] = ob
                    l_ref[pl.ds(qo, QBLK), cs] = lb
                    return carry
                lax.fori_loop(1, nblk, body, 0, unroll=ATTN_UNROLL)

    run_group(q0_ref, k0_ref, v0_ref, o0_ref, l0_ref, GROUPS[0][1])
    run_group(q1_ref, k1_ref, v1_ref, o1_ref, l1_ref, GROUPS[1][1])
    run_group(q2_ref, k2_ref, v2_ref, o2_ref, l2_ref, GROUPS[2][1])


def _attention(qkv, batch, seq):
    in_specs, out_shape, out_specs = [], [], []
    for _, dil in GROUPS:
        shp = (batch, seq // dil, dil * GROUP_WIDTH)
        spec = pl.BlockSpec((None, seq // dil, dil * GROUP_WIDTH), lambda b: (b, 0, 0))
        in_specs += [spec] * 3
        out_shape += [jax.ShapeDtypeStruct(shp, F32)] * 2
        out_specs += [spec, spec]
    outs = pl.pallas_call(
        _attn_kernel,
        grid=(batch,),
        in_specs=in_specs,
        out_specs=out_specs,
        out_shape=out_shape,
        scratch_shapes=[pltpu.VMEM((4 * QBLK, 2 * QBLK), F32), pltpu.VMEM((4 * QBLK, QBLK), F32)],
        compiler_params=pltpu.CompilerParams(
            dimension_semantics=("parallel",), vmem_limit_bytes=VMEM_LIMIT),
        name="dilated_attn",
    )(*qkv)
    return [outs[0], outs[2], outs[4]], [outs[1], outs[3], outs[5]]


def _sample_attn_step(b, q_ref, kf_ref, vf_ref, cache_refs, o_ref, l_ref, state_refs):
    qrow = q_ref[pl.ds(b, 1), :]
    krow = kf_ref[pl.ds(b, 1), :]
    vrow = vf_ref[pl.ds(b, 1), :]

    ri = lax.broadcasted_iota(jnp.int32, (GROUP_WIDTH, GROUP_WIDTH), 0)
    ci = lax.broadcasted_iota(jnp.int32, (GROUP_WIDTH, GROUP_WIDTH), 1)
    diag = ri == ci
    lane_head = lax.shift_right_logical(lax.broadcasted_iota(jnp.int32, (1, GROUP_WIDTH), 1), 6)

    def to_col(row):
        return jnp.sum(jnp.where(diag, jnp.broadcast_to(row, (GROUP_WIDTH, GROUP_WIDTH)), 0.0),
                       axis=1, keepdims=True)

    def to_row(col):
        return jnp.sum(jnp.where(diag, jnp.broadcast_to(col, (GROUP_WIDTH, GROUP_WIDTH)), 0.0),
                       axis=0, keepdims=True)

    def head_sum(x):
        return jnp.sum(x.reshape(HEADS_PER_GROUP, HEAD_DIM, x.shape[1]), axis=1)

    def head_row(x):
        out = jnp.zeros((1, GROUP_WIDTH), F32)
        for h in range(HEADS_PER_GROUP):
            out = jnp.where(lane_head == h, jnp.broadcast_to(x[h:h + 1, :], (1, GROUP_WIDTH)), out)
        return out

    for g, (win, dil) in enumerate(GROUPS):
        cs = slice(g * GROUP_WIDTH, (g + 1) * GROUP_WIDTH)
        ck_ref, cv_ref = cache_refs[2 * g], cache_refs[2 * g + 1]
        sk_ref, sv_ref = state_refs[2 * g], state_refs[2 * g + 1]
        length = ck_ref.shape[1]
        qc = to_col(qrow[:, cs])
        kc = to_col(krow[:, cs])
        vc = to_col(vrow[:, cs])

        kmat = ck_ref[...]
        s = head_sum(_round_bf16(kmat) * qc)
        lane = lax.broadcasted_iota(jnp.int32, (HEADS_PER_GROUP, length), 1)
        s = jnp.where(jnp.bitwise_and(lane, dil - 1) == 0, s, F32_MIN)
        s_new = head_sum(jnp.broadcast_to(_round_bf16(kc) * qc, (GROUP_WIDTH, LANES)))[:, 0:1]
        m = jnp.maximum(jnp.max(s, axis=1, keepdims=True), s_new)
        p = jnp.exp(s - m)
        p_new = jnp.exp(s_new - m)
        den = jnp.sum(p, axis=1, keepdims=True) + p_new

        vmat = cv_ref[...]
        pb = _round_bf16(p)
        p_full = jnp.broadcast_to(pb[:, None, :], (HEADS_PER_GROUP, HEAD_DIM, length)).reshape(GROUP_WIDTH, length)
        acc_c = jnp.sum(p_full * _round_bf16(vmat), axis=1, keepdims=True)
        acc_row = to_row(acc_c) + head_row(_round_bf16(p_new)) * _round_bf16(vrow[:, cs])
        den_row = head_row(den)
        o_ref[pl.ds(b, 1), cs] = acc_row / den_row
        l_ref[pl.ds(b, 1), cs] = head_row(m) + jnp.log(den_row)

        last = lax.broadcasted_iota(jnp.int32, (GROUP_WIDTH, length), 1) == length - 1
        sk_ref[...] = jnp.where(last, kc, pltpu.roll(kmat, length - 1, 1))
        sv_ref[...] = jnp.where(last, vc, pltpu.roll(vmat, length - 1, 1))


def _post_tail(os, ls, ga_ref, ct, ma_ref, mc_ref, x_ref, wao_ref, wco_ref, wo_ref, gpost_ref, y_ref):
    mx = jnp.maximum(jnp.maximum(ls[0], ls[1]), ls[2])
    es = [jnp.exp(l - mx) for l in ls]
    inv = 1.0 / (es[0] + es[1] + es[2])
    a = jnp.concatenate([os[g] * (es[g] * inv) for g in range(N_GROUPS)], axis=1)
    ya = jnp.dot((a * ga_ref[...].astype(F32)).astype(BF16), wao_ref[...], preferred_element_type=F32)
    yc = jnp.dot(ct, wco_ref[...], preferred_element_type=F32)
    merged = ma_ref[...].astype(F32) * ya + mc_ref[...].astype(F32) * yc
    z = jnp.dot(merged.astype(BF16), wo_ref[...], preferred_element_type=F32)
    ms = jnp.mean(z * z, axis=-1, keepdims=True)
    y_ref[...] = x_ref[...] + (z * lax.rsqrt(ms + RMS_EPS)) * gpost_ref[...]


N_POST_IN = 15
N_SAMPLE_IN = 9


def _post_prompt_kernel(*refs, n_alias):
    (o0_ref, o1_ref, o2_ref, l0_ref, l1_ref, l2_ref, ga_ref, ct_ref, ma_ref, mc_ref,
     x_ref, wao_ref, wco_ref, wo_ref, gpost_ref) = refs[:N_POST_IN]
    sq_ref, skf_ref, svf_ref = refs[N_POST_IN:N_POST_IN + 3]
    cache_refs = refs[N_POST_IN + 3:N_POST_IN + N_SAMPLE_IN]
    outs = refs[N_POST_IN + N_SAMPLE_IN + n_alias:]
    y_ref, so_ref, sl_ref = outs[0:3]
    state_refs = outs[3:9]
    nat_ref = outs[9]
    tm = x_ref.shape[0]

    seq_index = pl.program_id(0) * pl.num_programs(1) + pl.program_id(1)
    _sample_attn_step(seq_index, sq_ref, skf_ref, svf_ref, cache_refs, so_ref, sl_ref, state_refs)

    def natural(ref, slot, dil):
        if dil == 1:
            return ref[...]
        rows = tm // dil
        halves = GROUP_WIDTH // LANES
        for r in range(dil):
            for half in range(halves):
                c0 = r * GROUP_WIDTH + half * LANES
                nat_ref[slot, half, pl.ds(r, rows, stride=dil), :] = ref[:, c0:c0 + LANES]
        return jnp.concatenate([nat_ref[slot, half] for half in range(halves)], axis=1)

    os = [natural(o0_ref, 0, 1), natural(o1_ref, 0, GROUPS[1][1]), natural(o2_ref, 1, GROUPS[2][1])]
    ls = [natural(l0_ref, 0, 1), natural(l1_ref, 2, GROUPS[1][1]), natural(l2_ref, 3, GROUPS[2][1])]
    _post_tail(os, ls, ga_ref, ct_ref[...], ma_ref, mc_ref, x_ref, wao_ref, wco_ref, wo_ref, gpost_ref, y_ref)


def _post_sample_kernel(o_ref, l_ref, ga_ref, u_ref, st_ref, gc_ref, ma_ref, mc_ref, x_ref,
                        wao_ref, wco_ref, wo_ref, gpost_ref, cw_ref, lng_ref, lnb_ref, y_ref):
    cdw = cw_ref[CONV_K - 1:CONV_K, :] * u_ref[...]
    for j in range(CONV_K - 1):
        cdw = cdw + cw_ref[j:j + 1, :] * st_ref[j]
    mu, rstd = _ln_stats(cdw)
    ct = _conv_gate(cdw, mu, rstd, lng_ref[...], lnb_ref[...], gc_ref[...].astype(F32)).astype(BF16)
    os = [o_ref[:, g * GROUP_WIDTH:(g + 1) * GROUP_WIDTH] for g in range(N_GROUPS)]
    ls = [l_ref[:, g * GROUP_WIDTH:(g + 1) * GROUP_WIDTH] for g in range(N_GROUPS)]
    _post_tail(os, ls, ga_ref, ct, ma_ref, mc_ref, x_ref, wao_ref, wco_ref, wo_ref, gpost_ref, y_ref)


def _weight_specs():
    full2 = (lambda *i: (0, 0))
    return [
        pl.BlockSpec((ATTN_WIDTH, D_MODEL), full2, pipeline_mode=pl.Buffered(1)),
        pl.BlockSpec((CONV_WIDTH, D_MODEL), full2, pipeline_mode=pl.Buffered(1)),
        pl.BlockSpec((D_MODEL, D_MODEL), full2, pipeline_mode=pl.Buffered(1)),
        pl.BlockSpec((1, D_MODEL), full2),
    ]


def _post_prompt(o, l, ga, ct, ma, mc, x2d, weights, batch, seq, sq, skf, svf, caches_t, layer, prev_states):
    tm = TM_POST
    m = batch * seq
    nt = seq // tm
    db = sq.shape[0]
    depth = caches_t[0].shape[0]
    assert batch * nt == db
    row = lambda b, i: (b * nt + i, 0)
    dec = [pl.BlockSpec((None, tm // dil, dil * GROUP_WIDTH), lambda b, i: (b, i, 0)) for _, dil in GROUPS]
    in_specs = (dec + dec
                + [pl.BlockSpec((tm, ATTN_WIDTH), row),
                   pl.BlockSpec((tm, CONV_WIDTH), row),
                   pl.BlockSpec((tm, D_MODEL), row),
                   pl.BlockSpec((tm, D_MODEL), row),
                   pl.BlockSpec((tm, D_MODEL), row)]
                + _weight_specs())
    assert len(in_specs) == N_POST_IN
    full = pl.BlockSpec((db, ATTN_WIDTH), lambda b, i: (0, 0))
    in_specs += [full] * 3
    out_shape = [jax.ShapeDtypeStruct((m, D_MODEL), F32)] + [jax.ShapeDtypeStruct((db, ATTN_WIDTH), F32)] * 2
    out_specs = [pl.BlockSpec((tm, D_MODEL), row), full, full]
    for c in caches_t:
        length = c.shape[3]
        spec = pl.BlockSpec((None, None, GROUP_WIDTH, length), lambda b, i: (layer, b * nt + i, 0, 0))
        in_specs.append(spec)
        out_shape.append(jax.ShapeDtypeStruct((depth, db, GROUP_WIDTH, length), F32))
        out_specs.append(spec)
    n_alias = len(prev_states)
    in_specs += [pl.BlockSpec(memory_space=pl.ANY)] * n_alias
    aliases = {N_POST_IN + N_SAMPLE_IN + j: 3 + j for j in range(n_alias)}
    outs = pl.pallas_call(
        functools.partial(_post_prompt_kernel, n_alias=n_alias),
        grid=(batch, nt),
        in_specs=in_specs,
        out_specs=out_specs,
        out_shape=out_shape,
        scratch_shapes=[pltpu.VMEM((4, GROUP_WIDTH // LANES, tm, LANES), F32)],
        input_output_aliases=aliases,
        compiler_params=pltpu.CompilerParams(
            dimension_semantics=("arbitrary", "arbitrary"), vmem_limit_bytes=VMEM_LIMIT),
        name="post_prompt",
    )(*o, *l, ga, ct, ma, mc, x2d, *weights, sq, skf, svf, *caches_t, *prev_states)
    return outs[0], outs[1], outs[2], outs[3:9]


def _post_sample(o, l, ga, u, state_t, gc, ma, mc, x2d, weights, conv_w, ln_g, ln_b, layer):
    db = x2d.shape[0]
    row = lambda i: (0, 0)
    in_specs = ([pl.BlockSpec((db, ATTN_WIDTH), row)] * 3
                + [pl.BlockSpec((db, CONV_WIDTH), row),
                   pl.BlockSpec((None, CONV_K - 1, db, CONV_WIDTH), lambda i: (layer, 0, 0, 0)),
                   pl.BlockSpec((db, CONV_WIDTH), row),
                   pl.BlockSpec((db, D_MODEL), row),
                   pl.BlockSpec((db, D_MODEL), row),
                   pl.BlockSpec((db, D_MODEL), row)]
                + _weight_specs()
                + [pl.BlockSpec((CONV_K, CONV_WIDTH), row),
                   pl.BlockSpec((1, CONV_WIDTH), row),
                   pl.BlockSpec((1, CONV_WIDTH), row)])
    return pl.pallas_call(
        _post_sample_kernel,
        grid=(1,),
        in_specs=in_specs,
        out_specs=pl.BlockSpec((db, D_MODEL), row),
        out_shape=jax.ShapeDtypeStruct((db, D_MODEL), F32),
        compiler_params=pltpu.CompilerParams(
            dimension_semantics=("arbitrary",), vmem_limit_bytes=VMEM_LIMIT),
        name="post_sample",
    )(o, l, ga, u, state_t, gc, ma, mc, x2d, *weights, conv_w, ln_g, ln_b)


def _rope_tables(pos):
    inv = jnp.power(jnp.float32(ROPE_THETA), -jnp.arange(0, ROT_DIM, 2, dtype=F32) / ROT_DIM)
    ang = pos.astype(F32)[:, None] * inv[None, :]
    cos, sin = jnp.cos(ang), jnp.sin(ang)
    npos = pos.shape[0]
    half = ROT_DIM // 2
    pad = jnp.zeros((npos, HEAD_DIM - ROT_DIM), F32)
    zero = jnp.zeros((npos, half), F32)
    cos_h = jnp.concatenate([cos, cos, pad + 1.0], axis=1)
    sa_h = jnp.concatenate([-sin, zero, pad], axis=1)
    sb_h = jnp.concatenate([zero, sin, pad], axis=1)
    rep = lambda t: jnp.tile(t, (1, HEADS_PER_GROUP))
    return rep(cos_h), rep(sa_h), rep(sb_h)


def _positions_minor(x):
    depth, n, npos = x.shape[:3]
    return jnp.transpose(x, (0, 1, 3, 4, 2)).reshape(depth, n, GROUP_WIDTH, npos)


def _positions_major(x):
    depth, n, _, npos = x.shape
    return jnp.transpose(x.reshape(depth, n, HEADS_PER_GROUP, HEAD_DIM, npos), (0, 1, 4, 2, 3))


def kernel(x_prompt, x_sample, cache_k0, cache_v0, cache_k1, cache_v1, cache_k2, cache_v2, state_conv,
           w_in, w_attn_out, w_conv_out, w_out, conv_w, conv_ln_g, conv_ln_b, norm_pre, norm_post):
    batch, seq, _ = x_prompt.shape
    db, dseq, _ = x_sample.shape
    depth = w_in.shape[0]
    assert dseq == 1 and seq % TM_IN == 0 and seq >= GROUPS[-1][0]
    caches = (cache_k0, cache_v0, cache_k1, cache_v1, cache_k2, cache_v2)
    for g, (win, _) in enumerate(GROUPS):
        assert caches[2 * g].shape[2] == win and caches[2 * g + 1].shape[2] == win
    caches_t = [_positions_minor(c) for c in caches]
    state_t = jnp.transpose(state_conv, (0, 2, 1, 3))

    w_in_bf = w_in.astype(BF16)
    w_ao_bf = w_attn_out.astype(BF16)
    w_co_bf = w_conv_out.astype(BF16)
    w_o_bf = w_out.astype(BF16)

    tab_p = _rope_tables(jnp.arange(seq, dtype=jnp.int32))
    tab_s = _rope_tables(jnp.full((db,), PAST_LEN, dtype=jnp.int32))

    yp = x_prompt.reshape(batch * seq, D_MODEL)
    ys = x_sample.reshape(db, D_MODEL)
    p_states, s_states = (), ()
    pc, sc = [], []
    for layer in range(depth):
        g_pre = norm_pre[layer][None, :]
        cw = conv_w[layer]
        ln_g = conv_ln_g[layer][None, :]
        ln_b = conv_ln_b[layer][None, :]
        weights = (w_ao_bf[layer], w_co_bf[layer], w_o_bf[layer], norm_post[layer][None, :])

        qkv, p_states, (ga, ct, ma, mc), utail = _inproj_prompt(
            yp, g_pre, w_in_bf[layer], tab_p, cw, ln_g, ln_b, batch, seq, depth, layer, p_states)
        o, l = _attention(qkv, batch, seq)
        (sq, skf, svf), (sga, su, sgc, sma, smc) = _inproj_sample(ys, g_pre, w_in_bf[layer], tab_s)
        yp, so, sl, s_states = _post_prompt(o, l, ga, ct, ma, mc, yp, weights, batch, seq,
                                            sq, skf, svf, caches_t, layer, s_states)
        ys = _post_sample(so, sl, sga, su, state_t, sgc, sma, smc, ys, weights, cw, ln_g, ln_b, layer)
        pc.append(utail[:, CONV_HALO - (CONV_K - 1):])
        sc.append(jnp.concatenate([state_conv[layer][:, 1:], su[:, None, :]], axis=1))

    pk = [_positions_major(p_states[g]) for g in range(N_GROUPS)]
    pv = [_positions_major(p_states[N_GROUPS + g]) for g in range(N_GROUPS)]
    sk = [_positions_major(s_states[2 * g]) for g in range(N_GROUPS)]
    sv = [_positions_major(s_states[2 * g + 1]) for g in range(N_GROUPS)]
    return (yp.reshape(batch, seq, D_MODEL), ys.reshape(db, 1, D_MODEL),
            pk[0], pv[0], pk[1], pv[1], pk[2], pv[2], jnp.stack(pc),
            sk[0], sv[0], sk[1], sv[1], sk[2], sv[2], jnp.stack(sc))
```

```python
import functools

import jax
import jax.numpy as jnp
from jax import lax
from jax.experimental import pallas as pl
from jax.experimental.pallas import tpu as pltpu

D_MODEL = 1024
HEAD_DIM = 64
HEADS_PER_GROUP = 4
GROUPS = ((128, 1), (512, 4), (2048, 16))
N_GROUPS = len(GROUPS)
GROUP_WIDTH = HEADS_PER_GROUP * HEAD_DIM
ATTN_WIDTH = N_GROUPS * GROUP_WIDTH
CONV_WIDTH = 768
CONV_K = 31
ROT_DIM = 16
ROPE_THETA = 500000.0
QBLK = 128
ATTN_UNROLL = 15
RMS_EPS = 1e-6
LN_EPS = 1e-5
PAST_LEN = 8192
IN_COLS = 4 * ATTN_WIDTH + 3 * CONV_WIDTH + 2 * D_MODEL

COL_Q, COL_K, COL_V, COL_GA = 0, 768, 1536, 2304
COL_CA, COL_CB, COL_GC = 3072, 3840, 4608
COL_MA, COL_MC = 5376, 6400

SUBLANES = 8
LANES = 128
CONV_HALO = 32
CONV_BASE = CONV_HALO - (CONV_K - 1)
CONV_CHUNK = 64
TM_IN = 512
TM_POST = 512
VMEM_LIMIT = 56 * 1024 * 1024

F32 = jnp.float32
BF16 = jnp.bfloat16
F32_MIN = float(jnp.finfo(jnp.float32).min)
F32_MAX = float(jnp.finfo(jnp.float32).max)


def _sigmoid(x):
    return jax.nn.sigmoid(x)


def _silu(x):
    return x * jax.nn.sigmoid(x)


def _round_bf16(x):
    return x.astype(BF16).astype(F32)


def _ln_stats(cdw):
    mu = jnp.mean(cdw, axis=-1, keepdims=True)
    xc = cdw - mu
    var = jnp.mean(xc * xc, axis=-1, keepdims=True)
    return mu, lax.rsqrt(var + LN_EPS)


def _conv_gate(cdw, mu, rstd, lng, lnb, gc_act):
    return _silu((cdw - mu) * rstd * lng + lnb) * gc_act


def _inproj_body(x_ref, g_ref, w_ref, cos_ref, sa_ref, sb_ref,
                 emit_q, emit_k, emit_v, emit_u, between, emit_gc, ga_ref, ma_ref, mc_ref):
    x = x_ref[...]
    ms = jnp.mean(x * x, axis=-1, keepdims=True)
    h = (x * lax.rsqrt(ms + RMS_EPS)) * g_ref[...]
    hb = h.astype(BF16)

    def proj(c0):
        return jnp.dot(hb, w_ref[:, c0:c0 + GROUP_WIDTH], preferred_element_type=F32)

    cos = cos_ref[...]
    sa = sa_ref[...]
    sb = sb_ref[...]

    def rope(z):
        return (z * cos + pltpu.roll(z, GROUP_WIDTH - ROT_DIM // 2, 1) * sa
                + pltpu.roll(z, ROT_DIM // 2, 1) * sb)

    for c in range(CONV_WIDTH // GROUP_WIDTH):
        a = proj(COL_CA + c * GROUP_WIDTH)
        b = proj(COL_CB + c * GROUP_WIDTH)
        emit_u(c, a * _sigmoid(b))
    for g in range(N_GROUPS):
        between()
        emit_q(g, rope(proj(COL_Q + g * GROUP_WIDTH)) * (HEAD_DIM ** -0.5))
        emit_k(g, rope(proj(COL_K + g * GROUP_WIDTH)))
        emit_v(g, proj(COL_V + g * GROUP_WIDTH))
    between()
    for c in range(CONV_WIDTH // GROUP_WIDTH):
        cs = slice(c * GROUP_WIDTH, (c + 1) * GROUP_WIDTH)
        ga_ref[:, cs] = _silu(proj(COL_GA + c * GROUP_WIDTH)).astype(BF16)
    between()
    for c in range(D_MODEL // GROUP_WIDTH):
        cs = slice(c * GROUP_WIDTH, (c + 1) * GROUP_WIDTH)
        ma_ref[:, cs] = _sigmoid(proj(COL_MA + c * GROUP_WIDTH)).astype(BF16)
    between()
    for c in range(D_MODEL // GROUP_WIDTH):
        cs = slice(c * GROUP_WIDTH, (c + 1) * GROUP_WIDTH)
        mc_ref[:, cs] = _sigmoid(proj(COL_MC + c * GROUP_WIDTH)).astype(BF16)
    between()
    for c in range(CONV_WIDTH // GROUP_WIDTH):
        emit_gc(c, _silu(proj(COL_GC + c * GROUP_WIDTH)))


N_INPROJ_IN = 9


def _inproj_prompt_kernel(*refs, n_alias, tiles_per_seq):
    x_ref, g_ref, w_ref, cos_ref, sa_ref, sb_ref, cw_ref, lng_ref, lnb_ref = refs[:N_INPROJ_IN]
    outs = refs[N_INPROJ_IN + n_alias:]
    qkv_refs = outs[0:9]
    pk_refs = outs[9:12]
    pv_refs = outs[12:15]
    ga_ref, ct_ref, ma_ref, mc_ref, utail_ref = outs[15:20]
    tmp_ref, ext_ref, shift_ref, cdw_ref = outs[20:24]
    tm = x_ref.shape[0]

    def emit_decimated(ref, z, dil):
        if dil == 1:
            ref[...] = z.astype(BF16)
            return
        rows = tm // dil
        for half in range(GROUP_WIDTH // LANES):
            tmp_ref[half] = z[:, half * LANES:(half + 1) * LANES]
        for r in range(dil):
            for half in range(GROUP_WIDTH // LANES):
                c0 = r * GROUP_WIDTH + half * LANES
                ref[:, c0:c0 + LANES] = tmp_ref[half, pl.ds(r, rows, stride=dil), :].astype(BF16)

    def emit_state(ref, z):
        keep = ref.shape[1]
        ref[...] = z[tm - keep:, :].T

    def emit_q(g, z):
        emit_decimated(qkv_refs[3 * g], z, GROUPS[g][1])

    def emit_k(g, z):
        emit_decimated(qkv_refs[3 * g + 1], z, GROUPS[g][1])
        emit_state(pk_refs[g], z)

    def emit_v(g, z):
        emit_decimated(qkv_refs[3 * g + 2], z, GROUPS[g][1])
        emit_state(pv_refs[g], z)

    first = (pl.program_id(0) % tiles_per_seq) == 0
    halo = jnp.where(first, 0.0, ext_ref[tm:tm + CONV_HALO, :])
    ext_ref[0:CONV_HALO, :] = halo

    def emit_u(c, u):
        ext_ref[CONV_HALO:, c * GROUP_WIDTH:(c + 1) * GROUP_WIDTH] = u

    def conv_lane_block(lb):
        nshift = tm + CONV_HALO - SUBLANES
        nsub = CONV_CHUNK // SUBLANES
        lanes = slice(lb * LANES, (lb + 1) * LANES)
        buf = lb % 2
        for s in range(1, SUBLANES):
            shift_ref[buf, s - 1] = ext_ref[pl.ds(s, nshift), lanes]
        wts = [jnp.broadcast_to(cw_ref[j:j + 1, lanes], (SUBLANES, LANES)) for j in range(CONV_K)]
        for c0 in range(0, tm, CONV_CHUNK):
            acc = [jnp.zeros((SUBLANES, LANES), F32) for _ in range(nsub)]
            for j in range(CONV_K):
                off = CONV_BASE + j
                s, a = off % SUBLANES, off - off % SUBLANES
                if s == 0:
                    win = ext_ref[c0 + a:c0 + a + CONV_CHUNK, lanes]
                else:
                    win = shift_ref[buf, s - 1, c0 + a:c0 + a + CONV_CHUNK, :]
                for k in range(nsub):
                    acc[k] = acc[k] + wts[j] * win[k * SUBLANES:(k + 1) * SUBLANES, :]
            cdw_ref[c0:c0 + CONV_CHUNK, lanes] = jnp.concatenate(acc, axis=0)

    pending = list(range(CONV_WIDTH // LANES))

    def between():
        if len(pending) == CONV_WIDTH // LANES:
            utail_ref[...] = ext_ref[tm:tm + CONV_HALO, :]
        if pending:
            conv_lane_block(pending.pop(0))

    def emit_gc(c, gc_act):
        cs = slice(c * GROUP_WIDTH, (c + 1) * GROUP_WIDTH)
        assert not pending
        if c == 0:
            stats.extend(_ln_stats(cdw_ref[...]))
        mu, rstd = stats
        ct_ref[:, cs] = _conv_gate(cdw_ref[:, cs], mu, rstd, lng_ref[:, cs], lnb_ref[:, cs], gc_act).astype(BF16)

    stats = []
    _inproj_body(x_ref, g_ref, w_ref, cos_ref, sa_ref, sb_ref,
                 emit_q, emit_k, emit_v, emit_u, between, emit_gc, ga_ref, ma_ref, mc_ref)


def _inproj_sample_kernel(x_ref, g_ref, w_ref, cos_ref, sa_ref, sb_ref,
                          q_ref, kf_ref, vf_ref, ga_ref, u_ref, gc_ref, ma_ref, mc_ref):
    def emit_q(g, z):
        q_ref[:, g * GROUP_WIDTH:(g + 1) * GROUP_WIDTH] = _round_bf16(z)

    def emit_k(g, z):
        kf_ref[:, g * GROUP_WIDTH:(g + 1) * GROUP_WIDTH] = z

    def emit_v(g, z):
        vf_ref[:, g * GROUP_WIDTH:(g + 1) * GROUP_WIDTH] = z

    def emit_u(c, u):
        u_ref[:, c * GROUP_WIDTH:(c + 1) * GROUP_WIDTH] = u

    def emit_gc(c, gc_act):
        gc_ref[:, c * GROUP_WIDTH:(c + 1) * GROUP_WIDTH] = gc_act.astype(BF16)

    _inproj_body(x_ref, g_ref, w_ref, cos_ref, sa_ref, sb_ref,
                 emit_q, emit_k, emit_v, emit_u, lambda: None, emit_gc, ga_ref, ma_ref, mc_ref)


def _inproj_in_specs(tm, n_tab, layer):
    row = lambda i: (i, 0)
    full = lambda i: (0, 0)
    tab = lambda i: (i % n_tab, 0)
    return [
        pl.BlockSpec((tm, D_MODEL), row),
        pl.BlockSpec((1, D_MODEL), full),
        pl.BlockSpec((None, D_MODEL, IN_COLS), lambda i: (layer, 0, 0), pipeline_mode=pl.Buffered(1)),
        pl.BlockSpec((tm, GROUP_WIDTH), tab),
        pl.BlockSpec((tm, GROUP_WIDTH), tab),
        pl.BlockSpec((tm, GROUP_WIDTH), tab),
    ]


def _inproj_prompt(x2d, g_pre, w_in_bf, tabs, conv_w, ln_g, ln_b, batch, seq, depth, layer, prev_states):
    tm = TM_IN
    m = batch * seq
    nt = seq // tm
    row = lambda i: (i, 0)
    full = lambda i: (0, 0)
    in_specs = _inproj_in_specs(tm, nt, layer) + [
        pl.BlockSpec((CONV_K, CONV_WIDTH), full),
        pl.BlockSpec((1, CONV_WIDTH), full),
        pl.BlockSpec((1, CONV_WIDTH), full),
    ]
    assert len(in_specs) == N_INPROJ_IN
    n_alias = len(prev_states)
    in_specs += [pl.BlockSpec(memory_space=pl.ANY)] * n_alias

    out_shape, out_specs = [], []
    for _, dil in GROUPS:
        shp = (batch, seq // dil, dil * GROUP_WIDTH)
        spec = pl.BlockSpec((None, tm // dil, dil * GROUP_WIDTH), lambda i: (i // nt, i % nt, 0))
        out_shape += [jax.ShapeDtypeStruct(shp, BF16)] * 3
        out_specs += [spec] * 3
    for _ in range(2):
        for win, _ in GROUPS:
            keep = min(win, seq)
            blk = min(keep, tm)
            nblk = keep // blk
            imap = (lambda i, nblk=nblk:
                    (layer, i // nt, 0, jnp.maximum(i % nt - (nt - nblk), 0)))
            out_shape.append(jax.ShapeDtypeStruct((depth, batch, GROUP_WIDTH, keep), F32))
            out_specs.append(pl.BlockSpec((None, None, GROUP_WIDTH, blk), imap))
    out_shape += [jax.ShapeDtypeStruct((m, ATTN_WIDTH), BF16),
                  jax.ShapeDtypeStruct((m, CONV_WIDTH), BF16),
                  jax.ShapeDtypeStruct((m, D_MODEL), BF16),
                  jax.ShapeDtypeStruct((m, D_MODEL), BF16),
                  jax.ShapeDtypeStruct((batch, CONV_HALO, CONV_WIDTH), F32)]
    out_specs += [pl.BlockSpec((tm, ATTN_WIDTH), row),
                  pl.BlockSpec((tm, CONV_WIDTH), row),
                  pl.BlockSpec((tm, D_MODEL), row),
                  pl.BlockSpec((tm, D_MODEL), row),
                  pl.BlockSpec((None, CONV_HALO, CONV_WIDTH), lambda i: (i // nt, 0, 0))]
    aliases = {N_INPROJ_IN + j: 9 + j for j in range(n_alias)}
    outs = pl.pallas_call(
        functools.partial(_inproj_prompt_kernel, n_alias=n_alias, tiles_per_seq=nt),
        grid=(m // tm,),
        in_specs=in_specs,
        out_specs=out_specs,
        out_shape=out_shape,
        scratch_shapes=[pltpu.VMEM((GROUP_WIDTH // LANES, tm, LANES), F32),
                        pltpu.VMEM((tm + CONV_HALO, CONV_WIDTH), F32),
                        pltpu.VMEM((2, SUBLANES - 1, tm + CONV_HALO - SUBLANES, LANES), F32),
                        pltpu.VMEM((tm, CONV_WIDTH), F32)],
        input_output_aliases=aliases,
        compiler_params=pltpu.CompilerParams(
            dimension_semantics=("arbitrary",), vmem_limit_bytes=VMEM_LIMIT),
        name="inproj_prompt",
    )(x2d, g_pre, w_in_bf, *tabs, conv_w, ln_g, ln_b, *prev_states)
    return outs[0:9], outs[9:15], outs[15:19], outs[19]


def _inproj_sample(x2d, g_pre, w_in_bf, tabs, layer):
    m = x2d.shape[0]
    row = lambda i: (i, 0)
    out_shape = ([jax.ShapeDtypeStruct((m, ATTN_WIDTH), F32)] * 3
                 + [jax.ShapeDtypeStruct((m, ATTN_WIDTH), BF16),
                    jax.ShapeDtypeStruct((m, CONV_WIDTH), F32),
                    jax.ShapeDtypeStruct((m, CONV_WIDTH), BF16),
                    jax.ShapeDtypeStruct((m, D_MODEL), BF16),
                    jax.ShapeDtypeStruct((m, D_MODEL), BF16)])
    out_specs = ([pl.BlockSpec((m, ATTN_WIDTH), row)] * 5
                 + [pl.BlockSpec((m, CONV_WIDTH), row)]
                 + [pl.BlockSpec((m, D_MODEL), row)] * 2)
    outs = pl.pallas_call(
        _inproj_sample_kernel,
        grid=(1,),
        in_specs=_inproj_in_specs(m, 1, layer),
        out_specs=out_specs,
        out_shape=out_shape,
        compiler_params=pltpu.CompilerParams(
            dimension_semantics=("arbitrary",), vmem_limit_bytes=VMEM_LIMIT),
        name="inproj_sample",
    )(x2d, g_pre, w_in_bf, *tabs)
    return outs[0:3], outs[3:8]


def _head_masks():
    lane = lax.broadcasted_iota(jnp.int32, (1, GROUP_WIDTH), 1)
    return [lax.shift_right_logical(lane, 6) == h for h in range(HEADS_PER_GROUP)]


def _attend_block(q, k, v, cap, hmask):
    zero = jnp.zeros_like(q)
    qs = jnp.concatenate([jnp.where(hmask[h], q, zero) for h in range(HEADS_PER_GROUP)], axis=0)
    s = lax.dot_general(qs, k, (((1,), (1,)), ((), ())), preferred_element_type=F32)
    s = jnp.minimum(s, cap)
    m = jnp.max(s, axis=-1, keepdims=True)
    p = jnp.exp(s - m)
    den = jnp.sum(p, axis=-1, keepdims=True)
    r = jnp.dot(p.astype(BF16), v, preferred_element_type=F32)
    o = jnp.zeros((QBLK, GROUP_WIDTH), F32)
    mb = jnp.zeros((QBLK, GROUP_WIDTH), F32)
    db = jnp.ones((QBLK, GROUP_WIDTH), F32)
    for h in range(HEADS_PER_GROUP):
        rows = slice(h * QBLK, (h + 1) * QBLK)
        o = jnp.where(hmask[h], r[rows], o)
        mb = jnp.where(hmask[h], jnp.broadcast_to(m[rows], (QBLK, GROUP_WIDTH)), mb)
        db = jnp.where(hmask[h], jnp.broadcast_to(den[rows], (QBLK, GROUP_WIDTH)), db)
    return o / db, mb + jnp.log(db)


def _attn_kernel(q0_ref, k0_ref, v0_ref, q1_ref, k1_ref, v1_ref, q2_ref, k2_ref, v2_ref,
                 o0_ref, l0_ref, o1_ref, l1_ref, o2_ref, l2_ref, cap_band_ref, cap_first_ref):
    hmask = _head_masks()
    row = jnp.bitwise_and(lax.broadcasted_iota(jnp.int32, (4 * QBLK, 2 * QBLK), 0), QBLK - 1)
    col = lax.broadcasted_iota(jnp.int32, (4 * QBLK, 2 * QBLK), 1)
    cap_band_ref[...] = jnp.where((col >= row) & (col <= row + QBLK), F32_MAX, F32_MIN)
    row1 = jnp.bitwise_and(lax.broadcasted_iota(jnp.int32, (4 * QBLK, QBLK), 0), QBLK - 1)
    col1 = lax.broadcasted_iota(jnp.int32, (4 * QBLK, QBLK), 1)
    cap_first_ref[...] = jnp.where(col1 <= row1, F32_MAX, F32_MIN)

    def run_group(q_ref, k_ref, v_ref, o_ref, l_ref, dil):
        seq = q_ref.shape[0]
        nblk = seq // QBLK
        for r in range(dil):
            cs = slice(r * GROUP_WIDTH, (r + 1) * GROUP_WIDTH)
            o, l = _attend_block(q_ref[0:QBLK, cs], k_ref[0:QBLK, cs], v_ref[0:QBLK, cs],
                                 cap_first_ref[...], hmask)
            o_ref[0:QBLK, cs] = o.astype(BF16)
            l_ref[0:QBLK, cs] = l
            if nblk > 1:
                def body(j, carry):
                    qo = pl.multiple_of(j * QBLK, QBLK)
                    ko = pl.multiple_of((j - 1) * QBLK, QBLK)
                    ob, lb = _attend_block(q_ref[pl.ds(qo, QBLK), cs], k_ref[pl.ds(ko, 2 * QBLK), cs],
                                           v_ref[pl.ds(ko, 2 * QBLK), cs], cap_band_ref[...], hmask)
                    o_ref[pl.ds(qo, QBLK), cs] = ob.astype(BF16)
                    l_ref[pl.ds(qo, QBLK), cs] = lb
                    return carry
                lax.fori_loop(1, nblk, body, 0, unroll=ATTN_UNROLL)

    run_group(q0_ref, k0_ref, v0_ref, o0_ref, l0_ref, GROUPS[0][1])
    run_group(q1_ref, k1_ref, v1_ref, o1_ref, l1_ref, GROUPS[1][1])
    run_group(q2_ref, k2_ref, v2_ref, o2_ref, l2_ref, GROUPS[2][1])


def _attention(qkv, batch, seq):
    in_specs, out_shape, out_specs = [], [], []
    for _, dil in GROUPS:
        shp = (batch, seq // dil, dil * GROUP_WIDTH)
        spec = pl.BlockSpec((None, seq // dil, dil * GROUP_WIDTH), lambda b: (b, 0, 0))
        in_specs += [spec] * 3
        out_shape += [jax.ShapeDtypeStruct(shp, BF16), jax.ShapeDtypeStruct(shp, F32)]
        out_specs += [spec, spec]
    outs = pl.pallas_call(
        _attn_kernel,
        grid=(batch,),
        in_specs=in_specs,
        out_specs=out_specs,
        out_shape=out_shape,
        scratch_shapes=[pltpu.VMEM((4 * QBLK, 2 * QBLK), F32), pltpu.VMEM((4 * QBLK, QBLK), F32)],
        compiler_params=pltpu.CompilerParams(
            dimension_semantics=("parallel",), vmem_limit_bytes=VMEM_LIMIT),
        name="dilated_attn",
    )(*qkv)
    return [outs[0], outs[2], outs[4]], [outs[1], outs[3], outs[5]]


def _sample_attn_step(b, q_ref, kf_ref, vf_ref, cache_refs, o_ref, l_ref, state_refs):
    qrow = q_ref[pl.ds(b, 1), :]
    krow = kf_ref[pl.ds(b, 1), :]
    vrow = vf_ref[pl.ds(b, 1), :]

    ri = lax.broadcasted_iota(jnp.int32, (GROUP_WIDTH, GROUP_WIDTH), 0)
    ci = lax.broadcasted_iota(jnp.int32, (GROUP_WIDTH, GROUP_WIDTH), 1)
    diag = ri == ci
    lane_head = lax.shift_right_logical(lax.broadcasted_iota(jnp.int32, (1, GROUP_WIDTH), 1), 6)

    def to_col(row):
        return jnp.sum(jnp.where(diag, jnp.broadcast_to(row, (GROUP_WIDTH, GROUP_WIDTH)), 0.0),
                       axis=1, keepdims=True)

    def to_row(col):
        return jnp.sum(jnp.where(diag, jnp.broadcast_to(col, (GROUP_WIDTH, GROUP_WIDTH)), 0.0),
                       axis=0, keepdims=True)

    def head_sum(x):
        return jnp.sum(x.reshape(HEADS_PER_GROUP, HEAD_DIM, x.shape[1]), axis=1)

    def head_row(x):
        out = jnp.zeros((1, GROUP_WIDTH), F32)
        for h in range(HEADS_PER_GROUP):
            out = jnp.where(lane_head == h, jnp.broadcast_to(x[h:h + 1, :], (1, GROUP_WIDTH)), out)
        return out

    for g, (win, dil) in enumerate(GROUPS):
        cs = slice(g * GROUP_WIDTH, (g + 1) * GROUP_WIDTH)
        ck_ref, cv_ref = cache_refs[2 * g], cache_refs[2 * g + 1]
        sk_ref, sv_ref = state_refs[2 * g], state_refs[2 * g + 1]
        length = ck_ref.shape[1]
        qc = to_col(qrow[:, cs])
        kc = to_col(krow[:, cs])
        vc = to_col(vrow[:, cs])

        kmat = ck_ref[...]
        s = head_sum(_round_bf16(kmat) * qc)
        lane = lax.broadcasted_iota(jnp.int32, (HEADS_PER_GROUP, length), 1)
        s = jnp.where(jnp.bitwise_and(lane, dil - 1) == 0, s, F32_MIN)
        s_new = head_sum(jnp.broadcast_to(_round_bf16(kc) * qc, (GROUP_WIDTH, LANES)))[:, 0:1]
        m = jnp.maximum(jnp.max(s, axis=1, keepdims=True), s_new)
        p = jnp.exp(s - m)
        p_new = jnp.exp(s_new - m)
        den = jnp.sum(p, axis=1, keepdims=True) + p_new

        vmat = cv_ref[...]
        pb = _round_bf16(p)
        p_full = jnp.broadcast_to(pb[:, None, :], (HEADS_PER_GROUP, HEAD_DIM, length)).reshape(GROUP_WIDTH, length)
        acc_c = jnp.sum(p_full * _round_bf16(vmat), axis=1, keepdims=True)
        acc_row = to_row(acc_c) + head_row(_round_bf16(p_new)) * _round_bf16(vrow[:, cs])
        den_row = head_row(den)
        o_ref[pl.ds(b, 1), cs] = acc_row / den_row
        l_ref[pl.ds(b, 1), cs] = head_row(m) + jnp.log(den_row)

        last = lax.broadcasted_iota(jnp.int32, (GROUP_WIDTH, length), 1) == length - 1
        sk_ref[...] = jnp.where(last, kc, pltpu.roll(kmat, length - 1, 1))
        sv_ref[...] = jnp.where(last, vc, pltpu.roll(vmat, length - 1, 1))


def _post_tail(os, ls, ga_ref, ct, ma_ref, mc_ref, x_ref, wao_ref, wco_ref, wo_ref, gpost_ref, y_ref):
    mx = jnp.maximum(jnp.maximum(ls[0], ls[1]), ls[2])
    es = [jnp.exp(l - mx) for l in ls]
    inv = 1.0 / (es[0] + es[1] + es[2])
    a = jnp.concatenate([os[g] * (es[g] * inv) for g in range(N_GROUPS)], axis=1)
    ya = jnp.dot((a * ga_ref[...].astype(F32)).astype(BF16), wao_ref[...], preferred_element_type=F32)
    yc = jnp.dot(ct, wco_ref[...], preferred_element_type=F32)
    merged = ma_ref[...].astype(F32) * ya + mc_ref[...].astype(F32) * yc
    z = jnp.dot(merged.astype(BF16), wo_ref[...], preferred_element_type=F32)
    ms = jnp.mean(z * z, axis=-1, keepdims=True)
    y_ref[...] = x_ref[...] + (z * lax.rsqrt(ms + RMS_EPS)) * gpost_ref[...]


N_POST_IN = 15
N_SAMPLE_IN = 9


def _post_prompt_kernel(*refs, n_alias):
    (o0_ref, o1_ref, o2_ref, l0_ref, l1_ref, l2_ref, ga_ref, ct_ref, ma_ref, mc_ref,
     x_ref, wao_ref, wco_ref, wo_ref, gpost_ref) = refs[:N_POST_IN]
    sq_ref, skf_ref, svf_ref = refs[N_POST_IN:N_POST_IN + 3]
    cache_refs = refs[N_POST_IN + 3:N_POST_IN + N_SAMPLE_IN]
    outs = refs[N_POST_IN + N_SAMPLE_IN + n_alias:]
    y_ref, so_ref, sl_ref = outs[0:3]
    state_refs = outs[3:9]
    nat_ref = outs[9]
    tm = x_ref.shape[0]

    seq_index = pl.program_id(0) * pl.num_programs(1) + pl.program_id(1)
    _sample_attn_step(seq_index, sq_ref, skf_ref, svf_ref, cache_refs, so_ref, sl_ref, state_refs)

    def natural(ref, slot, dil):
        if dil == 1:
            return ref[...].astype(F32)
        rows = tm // dil
        halves = GROUP_WIDTH // LANES
        for r in range(dil):
            for half in range(halves):
                c0 = r * GROUP_WIDTH + half * LANES
                nat_ref[slot, half, pl.ds(r, rows, stride=dil), :] = ref[:, c0:c0 + LANES].astype(F32)
        return jnp.concatenate([nat_ref[slot, half] for half in range(halves)], axis=1)

    os = [natural(o0_ref, 0, 1), natural(o1_ref, 0, GROUPS[1][1]), natural(o2_ref, 1, GROUPS[2][1])]
    ls = [natural(l0_ref, 0, 1), natural(l1_ref, 2, GROUPS[1][1]), natural(l2_ref, 3, GROUPS[2][1])]
    _post_tail(os, ls, ga_ref, ct_ref[...], ma_ref, mc_ref, x_ref, wao_ref, wco_ref, wo_ref, gpost_ref, y_ref)


def _post_sample_kernel(o_ref, l_ref, ga_ref, u_ref, st_ref, gc_ref, ma_ref, mc_ref, x_ref,
                        wao_ref, wco_ref, wo_ref, gpost_ref, cw_ref, lng_ref, lnb_ref, y_ref):
    cdw = cw_ref[CONV_K - 1:CONV_K, :] * u_ref[...]
    for j in range(CONV_K - 1):
        cdw = cdw + cw_ref[j:j + 1, :] * st_ref[j]
    mu, rstd = _ln_stats(cdw)
    ct = _conv_gate(cdw, mu, rstd, lng_ref[...], lnb_ref[...], gc_ref[...].astype(F32)).astype(BF16)
    os = [o_ref[:, g * GROUP_WIDTH:(g + 1) * GROUP_WIDTH] for g in range(N_GROUPS)]
    ls = [l_ref[:, g * GROUP_WIDTH:(g + 1) * GROUP_WIDTH] for g in range(N_GROUPS)]
    _post_tail(os, ls, ga_ref, ct, ma_ref, mc_ref, x_ref, wao_ref, wco_ref, wo_ref, gpost_ref, y_ref)


def _weight_specs(layer):
    full2 = (lambda *i: (0, 0))
    slab = (lambda *i: (layer, 0, 0))
    return [
        pl.BlockSpec((None, ATTN_WIDTH, D_MODEL), slab, pipeline_mode=pl.Buffered(1)),
        pl.BlockSpec((None, CONV_WIDTH, D_MODEL), slab, pipeline_mode=pl.Buffered(1)),
        pl.BlockSpec((None, D_MODEL, D_MODEL), slab, pipeline_mode=pl.Buffered(1)),
        pl.BlockSpec((1, D_MODEL), full2),
    ]


def _post_prompt(o, l, ga, ct, ma, mc, x2d, weights, batch, seq, sq, skf, svf, caches_t, layer, prev_states):
    tm = TM_POST
    m = batch * seq
    nt = seq // tm
    db = sq.shape[0]
    depth = caches_t[0].shape[0]
    assert batch * nt == db
    row = lambda b, i: (b * nt + i, 0)
    dec = [pl.BlockSpec((None, tm // dil, dil * GROUP_WIDTH), lambda b, i: (b, i, 0)) for _, dil in GROUPS]
    in_specs = (dec + dec
                + [pl.BlockSpec((tm, ATTN_WIDTH), row),
                   pl.BlockSpec((tm, CONV_WIDTH), row),
                   pl.BlockSpec((tm, D_MODEL), row),
                   pl.BlockSpec((tm, D_MODEL), row),
                   pl.BlockSpec((tm, D_MODEL), row)]
                + _weight_specs(layer))
    assert len(in_specs) == N_POST_IN
    full = pl.BlockSpec((db, ATTN_WIDTH), lambda b, i: (0, 0))
    in_specs += [full] * 3
    out_shape = [jax.ShapeDtypeStruct((m, D_MODEL), F32)] + [jax.ShapeDtypeStruct((db, ATTN_WIDTH), F32)] * 2
    out_specs = [pl.BlockSpec((tm, D_MODEL), row), full, full]
    for c in caches_t:
        length = c.shape[3]
        spec = pl.BlockSpec((None, None, GROUP_WIDTH, length), lambda b, i: (layer, b * nt + i, 0, 0))
        in_specs.append(spec)
        out_shape.append(jax.ShapeDtypeStruct((depth, db, GROUP_WIDTH, length), F32))
        out_specs.append(spec)
    n_alias = len(prev_states)
    in_specs += [pl.BlockSpec(memory_space=pl.ANY)] * n_alias
    aliases = {N_POST_IN + N_SAMPLE_IN + j: 3 + j for j in range(n_alias)}
    outs = pl.pallas_call(
        functools.partial(_post_prompt_kernel, n_alias=n_alias),
        grid=(batch, nt),
        in_specs=in_specs,
        out_specs=out_specs,
        out_shape=out_shape,
        scratch_shapes=[pltpu.VMEM((4, GROUP_WIDTH // LANES, tm, LANES), F32)],
        input_output_aliases=aliases,
        compiler_params=pltpu.CompilerParams(
            dimension_semantics=("arbitrary", "arbitrary"), vmem_limit_bytes=VMEM_LIMIT),
        name="post_prompt",
    )(*o, *l, ga, ct, ma, mc, x2d, *weights, sq, skf, svf, *caches_t, *prev_states)
    return outs[0], outs[1], outs[2], outs[3:9]


def _post_sample(o, l, ga, u, state_t, gc, ma, mc, x2d, weights, conv_w, ln_g, ln_b, layer):
    db = x2d.shape[0]
    row = lambda i: (0, 0)
    in_specs = ([pl.BlockSpec((db, ATTN_WIDTH), row)] * 3
                + [pl.BlockSpec((db, CONV_WIDTH), row),
                   pl.BlockSpec((None, CONV_K - 1, db, CONV_WIDTH), lambda i: (layer, 0, 0, 0)),
                   pl.BlockSpec((db, CONV_WIDTH), row),
                   pl.BlockSpec((db, D_MODEL), row),
                   pl.BlockSpec((db, D_MODEL), row),
                   pl.BlockSpec((db, D_MODEL), row)]
                + _weight_specs(layer)
                + [pl.BlockSpec((CONV_K, CONV_WIDTH), row),
                   pl.BlockSpec((1, CONV_WIDTH), row),
                   pl.BlockSpec((1, CONV_WIDTH), row)])
    return pl.pallas_call(
        _post_sample_kernel,
        grid=(1,),
        in_specs=in_specs,
        out_specs=pl.BlockSpec((db, D_MODEL), row),
        out_shape=jax.ShapeDtypeStruct((db, D_MODEL), F32),
        compiler_params=pltpu.CompilerParams(
            dimension_semantics=("arbitrary",), vmem_limit_bytes=VMEM_LIMIT),
        name="post_sample",
    )(o, l, ga, u, state_t, gc, ma, mc, x2d, *weights, conv_w, ln_g, ln_b)


def _rope_tables(pos):
    inv = jnp.power(jnp.float32(ROPE_THETA), -jnp.arange(0, ROT_DIM, 2, dtype=F32) / ROT_DIM)
    ang = pos.astype(F32)[:, None] * inv[None, :]
    cos, sin = jnp.cos(ang), jnp.sin(ang)
    npos = pos.shape[0]
    half = ROT_DIM // 2
    pad = jnp.zeros((npos, HEAD_DIM - ROT_DIM), F32)
    zero = jnp.zeros((npos, half), F32)
    cos_h = jnp.concatenate([cos, cos, pad + 1.0], axis=1)
    sa_h = jnp.concatenate([-sin, zero, pad], axis=1)
    sb_h = jnp.concatenate([zero, sin, pad], axis=1)
    rep = lambda t: jnp.tile(t, (1, HEADS_PER_GROUP))
    return rep(cos_h), rep(sa_h), rep(sb_h)


def _positions_minor(x):
    depth, n, npos = x.shape[:3]
    return jnp.transpose(x, (0, 1, 3, 4, 2)).reshape(depth, n, GROUP_WIDTH, npos)


def _positions_major(x):
    depth, n, _, npos = x.shape
    return jnp.transpose(x.reshape(depth, n, HEADS_PER_GROUP, HEAD_DIM, npos), (0, 1, 4, 2, 3))


def kernel(x_prompt, x_sample, cache_k0, cache_v0, cache_k1, cache_v1, cache_k2, cache_v2, state_conv,
           w_in, w_attn_out, w_conv_out, w_out, conv_w, conv_ln_g, conv_ln_b, norm_pre, norm_post):
    batch, seq, _ = x_prompt.shape
    db, dseq, _ = x_sample.shape
    depth = w_in.shape[0]
    assert dseq == 1 and seq % TM_IN == 0 and seq >= GROUPS[-1][0]
    caches = (cache_k0, cache_v0, cache_k1, cache_v1, cache_k2, cache_v2)
    for g, (win, _) in enumerate(GROUPS):
        assert caches[2 * g].shape[2] == win and caches[2 * g + 1].shape[2] == win
    caches_t = [_positions_minor(c) for c in caches]
    state_t = jnp.transpose(state_conv, (0, 2, 1, 3))

    w_in_bf = w_in.astype(BF16)
    w_ao_bf = w_attn_out.astype(BF16)
    w_co_bf = w_conv_out.astype(BF16)
    w_o_bf = w_out.astype(BF16)

    tab_p = _rope_tables(jnp.arange(seq, dtype=jnp.int32))
    tab_s = _rope_tables(jnp.full((db,), PAST_LEN, dtype=jnp.int32))

    yp = x_prompt.reshape(batch * seq, D_MODEL)
    ys = x_sample.reshape(db, D_MODEL)
    p_states, s_states = (), ()
    pc, sc = [], []
    for layer in range(depth):
        g_pre = norm_pre[layer][None, :]
        cw = conv_w[layer]
        ln_g = conv_ln_g[layer][None, :]
        ln_b = conv_ln_b[layer][None, :]
        weights = (w_ao_bf, w_co_bf, w_o_bf, norm_post[layer][None, :])

        qkv, p_states, (ga, ct, ma, mc), utail = _inproj_prompt(
            yp, g_pre, w_in_bf, tab_p, cw, ln_g, ln_b, batch, seq, depth, layer, p_states)
        o, l = _attention(qkv, batch, seq)
        (sq, skf, svf), (sga, su, sgc, sma, smc) = _inproj_sample(ys, g_pre, w_in_bf, tab_s, layer)
        yp, so, sl, s_states = _post_prompt(o, l, ga, ct, ma, mc, yp, weights, batch, seq,
                                            sq, skf, svf, caches_t, layer, s_states)
        ys = _post_sample(so, sl, sga, su, state_t, sgc, sma, smc, ys, weights, cw, ln_g, ln_b, layer)
        pc.append(utail[:, CONV_HALO - (CONV_K - 1):])
        sc.append(jnp.concatenate([state_conv[layer][:, 1:], su[:, None, :]], axis=1))

    pk = [_positions_major(p_states[g]) for g in range(N_GROUPS)]
    pv = [_positions_major(p_states[N_GROUPS + g]) for g in range(N_GROUPS)]
    sk = [_positions_major(s_states[2 * g]) for g in range(N_GROUPS)]
    sv = [_positions_major(s_states[2 * g + 1]) for g in range(N_GROUPS)]
    return (yp.reshape(batch, seq, D_MODEL), ys.reshape(db, 1, D_MODEL),
            pk[0], pv[0], pk[1], pv[1], pk[2], pv[2], jnp.stack(pc),
            sk[0], sv[0], sk[1], sv[1], sk[2], sv[2], jnp.stack(sc))
```

```python
import functools

import jax
import jax.numpy as jnp
from jax import lax
from jax.experimental import pallas as pl
from jax.experimental.pallas import tpu as pltpu

D_MODEL = 1024
HEAD_DIM = 64
HEADS_PER_GROUP = 4
GROUPS = ((128, 1), (512, 4), (2048, 16))
N_GROUPS = len(GROUPS)
GROUP_WIDTH = HEADS_PER_GROUP * HEAD_DIM
ATTN_WIDTH = N_GROUPS * GROUP_WIDTH
CONV_WIDTH = 768
CONV_K = 31
ROT_DIM = 16
ROPE_THETA = 500000.0
QBLK = 128
ATTN_UNROLL = 15
RMS_EPS = 1e-6
LN_EPS = 1e-5
PAST_LEN = 8192
IN_COLS = 4 * ATTN_WIDTH + 3 * CONV_WIDTH + 2 * D_MODEL

COL_Q, COL_K, COL_V, COL_GA = 0, 768, 1536, 2304
COL_CA, COL_CB, COL_GC = 3072, 3840, 4608
COL_MA, COL_MC = 5376, 6400

SUBLANES = 8
LANES = 128
CONV_HALO = 32
CONV_BASE = CONV_HALO - (CONV_K - 1)
CONV_CHUNK = 128
TM_IN = 512
TM_POST = 512
VMEM_LIMIT = 56 * 1024 * 1024

F32 = jnp.float32
BF16 = jnp.bfloat16
F32_MIN = float(jnp.finfo(jnp.float32).min)
F32_MAX = float(jnp.finfo(jnp.float32).max)


def _sigmoid(x):
    return jax.nn.sigmoid(x)


def _silu(x):
    return x * jax.nn.sigmoid(x)


def _round_bf16(x):
    return x.astype(BF16).astype(F32)


def _ln_stats(cdw):
    mu = jnp.mean(cdw, axis=-1, keepdims=True)
    xc = cdw - mu
    var = jnp.mean(xc * xc, axis=-1, keepdims=True)
    return mu, lax.rsqrt(var + LN_EPS)


def _conv_gate(cdw, mu, rstd, lng, lnb, gc_act):
    return _silu((cdw - mu) * rstd * lng + lnb) * gc_act


def _inproj_body(x_ref, g_ref, w_ref, cos_ref, sa_ref, sb_ref,
                 emit_q, emit_k, emit_v, emit_u, between, emit_gc, ga_ref, ma_ref, mc_ref):
    x = x_ref[...]
    ms = jnp.mean(x * x, axis=-1, keepdims=True)
    h = (x * lax.rsqrt(ms + RMS_EPS)) * g_ref[...]
    hb = h.astype(BF16)

    def proj(c0):
        return jnp.dot(hb, w_ref[:, c0:c0 + GROUP_WIDTH], preferred_element_type=F32)

    cos = cos_ref[...]
    sa = sa_ref[...]
    sb = sb_ref[...]

    def rope(z):
        return (z * cos + pltpu.roll(z, GROUP_WIDTH - ROT_DIM // 2, 1) * sa
                + pltpu.roll(z, ROT_DIM // 2, 1) * sb)

    for c in range(CONV_WIDTH // GROUP_WIDTH):
        a = proj(COL_CA + c * GROUP_WIDTH)
        b = proj(COL_CB + c * GROUP_WIDTH)
        emit_u(c, a * _sigmoid(b))
    for g in range(N_GROUPS):
        between()
        emit_q(g, rope(proj(COL_Q + g * GROUP_WIDTH)) * (HEAD_DIM ** -0.5))
        emit_k(g, rope(proj(COL_K + g * GROUP_WIDTH)))
        emit_v(g, proj(COL_V + g * GROUP_WIDTH))
    between()
    for c in range(CONV_WIDTH // GROUP_WIDTH):
        cs = slice(c * GROUP_WIDTH, (c + 1) * GROUP_WIDTH)
        ga_ref[:, cs] = _silu(proj(COL_GA + c * GROUP_WIDTH)).astype(BF16)
    between()
    for c in range(D_MODEL // GROUP_WIDTH):
        cs = slice(c * GROUP_WIDTH, (c + 1) * GROUP_WIDTH)
        ma_ref[:, cs] = _sigmoid(proj(COL_MA + c * GROUP_WIDTH)).astype(BF16)
    between()
    for c in range(D_MODEL // GROUP_WIDTH):
        cs = slice(c * GROUP_WIDTH, (c + 1) * GROUP_WIDTH)
        mc_ref[:, cs] = _sigmoid(proj(COL_MC + c * GROUP_WIDTH)).astype(BF16)
    between()
    for c in range(CONV_WIDTH // GROUP_WIDTH):
        emit_gc(c, _silu(proj(COL_GC + c * GROUP_WIDTH)))


N_INPROJ_IN = 9


def _inproj_prompt_kernel(*refs, n_alias, tiles_per_seq):
    x_ref, g_ref, w_ref, cos_ref, sa_ref, sb_ref, cw_ref, lng_ref, lnb_ref = refs[:N_INPROJ_IN]
    outs = refs[N_INPROJ_IN + n_alias:]
    qkv_refs = outs[0:9]
    pk_refs = outs[9:12]
    pv_refs = outs[12:15]
    ga_ref, ct_ref, ma_ref, mc_ref, utail_ref = outs[15:20]
    tmp_ref, ext_ref, shift_ref, cdw_ref = outs[20:24]
    tm = x_ref.shape[0]

    def emit_decimated(ref, z, dil):
        if dil == 1:
            ref[...] = z.astype(BF16)
            return
        rows = tm // dil
        for half in range(GROUP_WIDTH // LANES):
            tmp_ref[half] = z[:, half * LANES:(half + 1) * LANES]
        for r in range(dil):
            for half in range(GROUP_WIDTH // LANES):
                c0 = r * GROUP_WIDTH + half * LANES
                ref[:, c0:c0 + LANES] = tmp_ref[half, pl.ds(r, rows, stride=dil), :].astype(BF16)

    def emit_state(ref, z):
        keep = ref.shape[1]
        ref[...] = z[tm - keep:, :].T

    def emit_q(g, z):
        emit_decimated(qkv_refs[3 * g], z, GROUPS[g][1])

    def emit_k(g, z):
        emit_decimated(qkv_refs[3 * g + 1], z, GROUPS[g][1])
        emit_state(pk_refs[g], z)

    def emit_v(g, z):
        emit_decimated(qkv_refs[3 * g + 2], z, GROUPS[g][1])
        emit_state(pv_refs[g], z)

    first = (pl.program_id(0) % tiles_per_seq) == 0
    halo = jnp.where(first, 0.0, ext_ref[tm:tm + CONV_HALO, :])
    ext_ref[0:CONV_HALO, :] = halo

    def emit_u(c, u):
        ext_ref[CONV_HALO:, c * GROUP_WIDTH:(c + 1) * GROUP_WIDTH] = u

    def conv_lane_block(lb):
        nshift = tm + CONV_HALO - SUBLANES
        nsub = CONV_CHUNK // SUBLANES
        lanes = slice(lb * LANES, (lb + 1) * LANES)
        buf = lb % 2
        for s in range(1, SUBLANES):
            shift_ref[buf, s - 1] = ext_ref[pl.ds(s, nshift), lanes]
        wts = [jnp.broadcast_to(cw_ref[j:j + 1, lanes], (SUBLANES, LANES)) for j in range(CONV_K)]
        for c0 in range(0, tm, CONV_CHUNK):
            acc = [jnp.zeros((SUBLANES, LANES), F32) for _ in range(nsub)]
            for j in range(CONV_K):
                off = CONV_BASE + j
                s, a = off % SUBLANES, off - off % SUBLANES
                if s == 0:
                    win = ext_ref[c0 + a:c0 + a + CONV_CHUNK, lanes]
                else:
                    win = shift_ref[buf, s - 1, c0 + a:c0 + a + CONV_CHUNK, :]
                for k in range(nsub):
                    acc[k] = acc[k] + wts[j] * win[k * SUBLANES:(k + 1) * SUBLANES, :]
            cdw_ref[c0:c0 + CONV_CHUNK, lanes] = jnp.concatenate(acc, axis=0)

    pending = list(range(CONV_WIDTH // LANES))

    def between():
        if len(pending) == CONV_WIDTH // LANES:
            utail_ref[...] = ext_ref[tm:tm + CONV_HALO, :]
        if pending:
            conv_lane_block(pending.pop(0))

    def emit_gc(c, gc_act):
        cs = slice(c * GROUP_WIDTH, (c + 1) * GROUP_WIDTH)
        assert not pending
        if c == 0:
            stats.extend(_ln_stats(cdw_ref[...]))
        mu, rstd = stats
        ct_ref[:, cs] = _conv_gate(cdw_ref[:, cs], mu, rstd, lng_ref[:, cs], lnb_ref[:, cs], gc_act).astype(BF16)

    stats = []
    _inproj_body(x_ref, g_ref, w_ref, cos_ref, sa_ref, sb_ref,
                 emit_q, emit_k, emit_v, emit_u, between, emit_gc, ga_ref, ma_ref, mc_ref)


def _inproj_sample_kernel(x_ref, g_ref, w_ref, cos_ref, sa_ref, sb_ref,
                          q_ref, kf_ref, vf_ref, ga_ref, u_ref, gc_ref, ma_ref, mc_ref):
    def emit_q(g, z):
        q_ref[:, g * GROUP_WIDTH:(g + 1) * GROUP_WIDTH] = _round_bf16(z)

    def emit_k(g, z):
        kf_ref[:, g * GROUP_WIDTH:(g + 1) * GROUP_WIDTH] = z

    def emit_v(g, z):
        vf_ref[:, g * GROUP_WIDTH:(g + 1) * GROUP_WIDTH] = z

    def emit_u(c, u):
        u_ref[:, c * GROUP_WIDTH:(c + 1) * GROUP_WIDTH] = u

    def emit_gc(c, gc_act):
        gc_ref[:, c * GROUP_WIDTH:(c + 1) * GROUP_WIDTH] = gc_act.astype(BF16)

    _inproj_body(x_ref, g_ref, w_ref, cos_ref, sa_ref, sb_ref,
                 emit_q, emit_k, emit_v, emit_u, lambda: None, emit_gc, ga_ref, ma_ref, mc_ref)


def _inproj_in_specs(tm, n_tab, layer):
    row = lambda i: (i, 0)
    full = lambda i: (0, 0)
    tab = lambda i: (i % n_tab, 0)
    return [
        pl.BlockSpec((tm, D_MODEL), row),
        pl.BlockSpec((1, D_MODEL), full),
        pl.BlockSpec((None, D_MODEL, IN_COLS), lambda i: (layer, 0, 0), pipeline_mode=pl.Buffered(1)),
        pl.BlockSpec((tm, GROUP_WIDTH), tab),
        pl.BlockSpec((tm, GROUP_WIDTH), tab),
        pl.BlockSpec((tm, GROUP_WIDTH), tab),
    ]


def _inproj_prompt(x2d, g_pre, w_in_bf, tabs, conv_w, ln_g, ln_b, batch, seq, depth, layer, prev_states):
    tm = TM_IN
    m = batch * seq
    nt = seq // tm
    row = lambda i: (i, 0)
    full = lambda i: (0, 0)
    in_specs = _inproj_in_specs(tm, nt, layer) + [
        pl.BlockSpec((CONV_K, CONV_WIDTH), full),
        pl.BlockSpec((1, CONV_WIDTH), full),
        pl.BlockSpec((1, CONV_WIDTH), full),
    ]
    assert len(in_specs) == N_INPROJ_IN
    n_alias = len(prev_states)
    in_specs += [pl.BlockSpec(memory_space=pl.ANY)] * n_alias

    out_shape, out_specs = [], []
    for _, dil in GROUPS:
        shp = (batch, seq // dil, dil * GROUP_WIDTH)
        spec = pl.BlockSpec((None, tm // dil, dil * GROUP_WIDTH), lambda i: (i // nt, i % nt, 0))
        out_shape += [jax.ShapeDtypeStruct(shp, BF16)] * 3
        out_specs += [spec] * 3
    for _ in range(2):
        for win, _ in GROUPS:
            keep = min(win, seq)
            blk = min(keep, tm)
            nblk = keep // blk
            imap = (lambda i, nblk=nblk:
                    (layer, i // nt, 0, jnp.maximum(i % nt - (nt - nblk), 0)))
            out_shape.append(jax.ShapeDtypeStruct((depth, batch, GROUP_WIDTH, keep), F32))
            out_specs.append(pl.BlockSpec((None, None, GROUP_WIDTH, blk), imap))
    out_shape += [jax.ShapeDtypeStruct((m, ATTN_WIDTH), BF16),
                  jax.ShapeDtypeStruct((m, CONV_WIDTH), BF16),
                  jax.ShapeDtypeStruct((m, D_MODEL), BF16),
                  jax.ShapeDtypeStruct((m, D_MODEL), BF16),
                  jax.ShapeDtypeStruct((batch, CONV_HALO, CONV_WIDTH), F32)]
    out_specs += [pl.BlockSpec((tm, ATTN_WIDTH), row),
                  pl.BlockSpec((tm, CONV_WIDTH), row),
                  pl.BlockSpec((tm, D_MODEL), row),
                  pl.BlockSpec((tm, D_MODEL), row),
                  pl.BlockSpec((None, CONV_HALO, CONV_WIDTH), lambda i: (i // nt, 0, 0))]
    aliases = {N_INPROJ_IN + j: 9 + j for j in range(n_alias)}
    outs = pl.pallas_call(
        functools.partial(_inproj_prompt_kernel, n_alias=n_alias, tiles_per_seq=nt),
        grid=(m // tm,),
        in_specs=in_specs,
        out_specs=out_specs,
        out_shape=out_shape,
        scratch_shapes=[pltpu.VMEM((GROUP_WIDTH // LANES, tm, LANES), F32),
                        pltpu.VMEM((tm + CONV_HALO, CONV_WIDTH), F32),
                        pltpu.VMEM((2, SUBLANES - 1, tm + CONV_HALO - SUBLANES, LANES), F32),
                        pltpu.VMEM((tm, CONV_WIDTH), F32)],
        input_output_aliases=aliases,
        compiler_params=pltpu.CompilerParams(
            dimension_semantics=("arbitrary",), vmem_limit_bytes=VMEM_LIMIT),
        name="inproj_prompt",
    )(x2d, g_pre, w_in_bf, *tabs, conv_w, ln_g, ln_b, *prev_states)
    return outs[0:9], outs[9:15], outs[15:19], outs[19]


def _inproj_sample(x2d, g_pre, w_in_bf, tabs, layer):
    m = x2d.shape[0]
    row = lambda i: (i, 0)
    out_shape = ([jax.ShapeDtypeStruct((m, ATTN_WIDTH), F32)] * 3
                 + [jax.ShapeDtypeStruct((m, ATTN_WIDTH), BF16),
                    jax.ShapeDtypeStruct((m, CONV_WIDTH), F32),
                    jax.ShapeDtypeStruct((m, CONV_WIDTH), BF16),
                    jax.ShapeDtypeStruct((m, D_MODEL), BF16),
                    jax.ShapeDtypeStruct((m, D_MODEL), BF16)])
    out_specs = ([pl.BlockSpec((m, ATTN_WIDTH), row)] * 5
                 + [pl.BlockSpec((m, CONV_WIDTH), row)]
                 + [pl.BlockSpec((m, D_MODEL), row)] * 2)
    outs = pl.pallas_call(
        _inproj_sample_kernel,
        grid=(1,),
        in_specs=_inproj_in_specs(m, 1, layer),
        out_specs=out_specs,
        out_shape=out_shape,
        compiler_params=pltpu.CompilerParams(
            dimension_semantics=("arbitrary",), vmem_limit_bytes=VMEM_LIMIT),
        name="inproj_sample",
    )(x2d, g_pre, w_in_bf, *tabs)
    return outs[0:3], outs[3:8]


def _head_masks():
    lane = lax.broadcasted_iota(jnp.int32, (1, GROUP_WIDTH), 1)
    return [lax.shift_right_logical(lane, 6) == h for h in range(HEADS_PER_GROUP)]


def _attend_block(q, k, v, cap, hmask):
    zero = jnp.zeros_like(q)
    qs = jnp.concatenate([jnp.where(hmask[h], q, zero) for h in range(HEADS_PER_GROUP)], axis=0)
    s = lax.dot_general(qs, k, (((1,), (1,)), ((), ())), preferred_element_type=F32)
    s = jnp.minimum(s, cap)
    m = jnp.max(s, axis=-1, keepdims=True)
    p = jnp.exp(s - m)
    den = jnp.sum(p, axis=-1, keepdims=True)
    r = jnp.dot(p.astype(BF16), v, preferred_element_type=F32)
    o = jnp.zeros((QBLK, GROUP_WIDTH), F32)
    mb = jnp.zeros((QBLK, GROUP_WIDTH), F32)
    db = jnp.ones((QBLK, GROUP_WIDTH), F32)
    for h in range(HEADS_PER_GROUP):
        rows = slice(h * QBLK, (h + 1) * QBLK)
        o = jnp.where(hmask[h], r[rows], o)
        mb = jnp.where(hmask[h], jnp.broadcast_to(m[rows], (QBLK, GROUP_WIDTH)), mb)
        db = jnp.where(hmask[h], jnp.broadcast_to(den[rows], (QBLK, GROUP_WIDTH)), db)
    return o / db, mb + jnp.log(db)


def _attn_kernel(q0_ref, k0_ref, v0_ref, q1_ref, k1_ref, v1_ref, q2_ref, k2_ref, v2_ref,
                 o0_ref, l0_ref, o1_ref, l1_ref, o2_ref, l2_ref, cap_band_ref, cap_first_ref):
    hmask = _head_masks()
    row = jnp.bitwise_and(lax.broadcasted_iota(jnp.int32, (4 * QBLK, 2 * QBLK), 0), QBLK - 1)
    col = lax.broadcasted_iota(jnp.int32, (4 * QBLK, 2 * QBLK), 1)
    cap_band_ref[...] = jnp.where((col >= row) & (col <= row + QBLK), F32_MAX, F32_MIN)
    row1 = jnp.bitwise_and(lax.broadcasted_iota(jnp.int32, (4 * QBLK, QBLK), 0), QBLK - 1)
    col1 = lax.broadcasted_iota(jnp.int32, (4 * QBLK, QBLK), 1)
    cap_first_ref[...] = jnp.where(col1 <= row1, F32_MAX, F32_MIN)

    def run_group(q_ref, k_ref, v_ref, o_ref, l_ref, dil):
        seq = q_ref.shape[0]
        nblk = seq // QBLK
        for r in range(dil):
            cs = slice(r * GROUP_WIDTH, (r + 1) * GROUP_WIDTH)
            o, l = _attend_block(q_ref[0:QBLK, cs], k_ref[0:QBLK, cs], v_ref[0:QBLK, cs],
                                 cap_first_ref[...], hmask)
            o_ref[0:QBLK, cs] = o.astype(BF16)
            l_ref[0:QBLK, cs] = l
            if nblk > 1:
                def body(j, carry):
                    qo = pl.multiple_of(j * QBLK, QBLK)
                    ko = pl.multiple_of((j - 1) * QBLK, QBLK)
                    ob, lb = _attend_block(q_ref[pl.ds(qo, QBLK), cs], k_ref[pl.ds(ko, 2 * QBLK), cs],
                                           v_ref[pl.ds(ko, 2 * QBLK), cs], cap_band_ref[...], hmask)
                    o_ref[pl.ds(qo, QBLK), cs] = ob.astype(BF16)
                    l_ref[pl.ds(qo, QBLK), cs] = lb
                    return carry
                lax.fori_loop(1, nblk, body, 0, unroll=ATTN_UNROLL)

    run_group(q0_ref, k0_ref, v0_ref, o0_ref, l0_ref, GROUPS[0][1])
    run_group(q1_ref, k1_ref, v1_ref, o1_ref, l1_ref, GROUPS[1][1])
    run_group(q2_ref, k2_ref, v2_ref, o2_ref, l2_ref, GROUPS[2][1])


def _attention(qkv, batch, seq):
    in_specs, out_shape, out_specs = [], [], []
    for _, dil in GROUPS:
        shp = (batch, seq // dil, dil * GROUP_WIDTH)
        spec = pl.BlockSpec((None, seq // dil, dil * GROUP_WIDTH), lambda b: (b, 0, 0))
        in_specs += [spec] * 3
        out_shape += [jax.ShapeDtypeStruct(shp, BF16), jax.ShapeDtypeStruct(shp, F32)]
        out_specs += [spec, spec]
    outs = pl.pallas_call(
        _attn_kernel,
        grid=(batch,),
        in_specs=in_specs,
        out_specs=out_specs,
        out_shape=out_shape,
        scratch_shapes=[pltpu.VMEM((4 * QBLK, 2 * QBLK), F32), pltpu.VMEM((4 * QBLK, QBLK), F32)],
        compiler_params=pltpu.CompilerParams(
            dimension_semantics=("parallel",), vmem_limit_bytes=VMEM_LIMIT),
        name="dilated_attn",
    )(*qkv)
    return [outs[0], outs[2], outs[4]], [outs[1], outs[3], outs[5]]


def _sample_attn_step(b, q_ref, kf_ref, vf_ref, cache_refs, o_ref, l_ref, state_refs):
    qrow = q_ref[pl.ds(b, 1), :]
    krow = kf_ref[pl.ds(b, 1), :]
    vrow = vf_ref[pl.ds(b, 1), :]

    ri = lax.broadcasted_iota(jnp.int32, (GROUP_WIDTH, GROUP_WIDTH), 0)
    ci = lax.broadcasted_iota(jnp.int32, (GROUP_WIDTH, GROUP_WIDTH), 1)
    diag = ri == ci
    lane_head = lax.shift_right_logical(lax.broadcasted_iota(jnp.int32, (1, GROUP_WIDTH), 1), 6)

    def to_col(row):
        return jnp.sum(jnp.where(diag, jnp.broadcast_to(row, (GROUP_WIDTH, GROUP_WIDTH)), 0.0),
                       axis=1, keepdims=True)

    def to_row(col):
        return jnp.sum(jnp.where(diag, jnp.broadcast_to(col, (GROUP_WIDTH, GROUP_WIDTH)), 0.0),
                       axis=0, keepdims=True)

    def head_sum(x):
        return jnp.sum(x.reshape(HEADS_PER_GROUP, HEAD_DIM, x.shape[1]), axis=1)

    def head_row(x):
        out = jnp.zeros((1, GROUP_WIDTH), F32)
        for h in range(HEADS_PER_GROUP):
            out = jnp.where(lane_head == h, jnp.broadcast_to(x[h:h + 1, :], (1, GROUP_WIDTH)), out)
        return out

    for g, (win, dil) in enumerate(GROUPS):
        cs = slice(g * GROUP_WIDTH, (g + 1) * GROUP_WIDTH)
        ck_ref, cv_ref = cache_refs[2 * g], cache_refs[2 * g + 1]
        sk_ref, sv_ref = state_refs[2 * g], state_refs[2 * g + 1]
        length = ck_ref.shape[1]
        qc = to_col(qrow[:, cs])
        kc = to_col(krow[:, cs])
        vc = to_col(vrow[:, cs])

        kmat = ck_ref[...]
        s = head_sum(_round_bf16(kmat) * qc)
        lane = lax.broadcasted_iota(jnp.int32, (HEADS_PER_GROUP, length), 1)
        s = jnp.where(jnp.bitwise_and(lane, dil - 1) == 0, s, F32_MIN)
        s_new = head_sum(jnp.broadcast_to(_round_bf16(kc) * qc, (GROUP_WIDTH, LANES)))[:, 0:1]
        m = jnp.maximum(jnp.max(s, axis=1, keepdims=True), s_new)
        p = jnp.exp(s - m)
        p_new = jnp.exp(s_new - m)
        den = jnp.sum(p, axis=1, keepdims=True) + p_new

        vmat = cv_ref[...]
        pb = _round_bf16(p)
        p_full = jnp.broadcast_to(pb[:, None, :], (HEADS_PER_GROUP, HEAD_DIM, length)).reshape(GROUP_WIDTH, length)
        acc_c = jnp.sum(p_full * _round_bf16(vmat), axis=1, keepdims=True)
        acc_row = to_row(acc_c) + head_row(_round_bf16(p_new)) * _round_bf16(vrow[:, cs])
        den_row = head_row(den)
        o_ref[pl.ds(b, 1), cs] = acc_row / den_row
        l_ref[pl.ds(b, 1), cs] = head_row(m) + jnp.log(den_row)

        last = lax.broadcasted_iota(jnp.int32, (GROUP_WIDTH, length), 1) == length - 1
        sk_ref[...] = jnp.where(last, kc, pltpu.roll(kmat, length - 1, 1))
        sv_ref[...] = jnp.where(last, vc, pltpu.roll(vmat, length - 1, 1))


def _post_tail(os, ls, ga_ref, ct, ma_ref, mc_ref, x_ref, wao_ref, wco_ref, wo_ref, gpost_ref, y_ref):
    mx = jnp.maximum(jnp.maximum(ls[0], ls[1]), ls[2])
    es = [jnp.exp(l - mx) for l in ls]
    inv = 1.0 / (es[0] + es[1] + es[2])
    a = jnp.concatenate([os[g] * (es[g] * inv) for g in range(N_GROUPS)], axis=1)
    ya = jnp.dot((a * ga_ref[...].astype(F32)).astype(BF16), wao_ref[...], preferred_element_type=F32)
    yc = jnp.dot(ct, wco_ref[...], preferred_element_type=F32)
    merged = ma_ref[...].astype(F32) * ya + mc_ref[...].astype(F32) * yc
    z = jnp.dot(merged.astype(BF16), wo_ref[...], preferred_element_type=F32)
    ms = jnp.mean(z * z, axis=-1, keepdims=True)
    y_ref[...] = x_ref[...] + (z * lax.rsqrt(ms + RMS_EPS)) * gpost_ref[...]


N_POST_IN = 15
N_SAMPLE_IN = 9


def _post_prompt_kernel(*refs, n_alias):
    (o0_ref, o1_ref, o2_ref, l0_ref, l1_ref, l2_ref, ga_ref, ct_ref, ma_ref, mc_ref,
     x_ref, wao_ref, wco_ref, wo_ref, gpost_ref) = refs[:N_POST_IN]
    sq_ref, skf_ref, svf_ref = refs[N_POST_IN:N_POST_IN + 3]
    cache_refs = refs[N_POST_IN + 3:N_POST_IN + N_SAMPLE_IN]
    outs = refs[N_POST_IN + N_SAMPLE_IN + n_alias:]
    y_ref, so_ref, sl_ref = outs[0:3]
    state_refs = outs[3:9]
    nat_ref = outs[9]
    tm = x_ref.shape[0]

    seq_index = pl.program_id(0) * pl.num_programs(1) + pl.program_id(1)
    _sample_attn_step(seq_index, sq_ref, skf_ref, svf_ref, cache_refs, so_ref, sl_ref, state_refs)

    def natural(ref, slot, dil):
        if dil == 1:
            return ref[...].astype(F32)
        rows = tm // dil
        halves = GROUP_WIDTH // LANES
        for r in range(dil):
            for half in range(halves):
                c0 = r * GROUP_WIDTH + half * LANES
                nat_ref[slot, half, pl.ds(r, rows, stride=dil), :] = ref[:, c0:c0 + LANES].astype(F32)
        return jnp.concatenate([nat_ref[slot, half] for half in range(halves)], axis=1)

    os = [natural(o0_ref, 0, 1), natural(o1_ref, 0, GROUPS[1][1]), natural(o2_ref, 1, GROUPS[2][1])]
    ls = [natural(l0_ref, 0, 1), natural(l1_ref, 2, GROUPS[1][1]), natural(l2_ref, 3, GROUPS[2][1])]
    _post_tail(os, ls, ga_ref, ct_ref[...], ma_ref, mc_ref, x_ref, wao_ref, wco_ref, wo_ref, gpost_ref, y_ref)


def _post_sample_kernel(o_ref, l_ref, ga_ref, u_ref, st_ref, gc_ref, ma_ref, mc_ref, x_ref,
                        wao_ref, wco_ref, wo_ref, gpost_ref, cw_ref, lng_ref, lnb_ref, y_ref):
    cdw = cw_ref[CONV_K - 1:CONV_K, :] * u_ref[...]
    for j in range(CONV_K - 1):
        cdw = cdw + cw_ref[j:j + 1, :] * st_ref[j]
    mu, rstd = _ln_stats(cdw)
    ct = _conv_gate(cdw, mu, rstd, lng_ref[...], lnb_ref[...], gc_ref[...].astype(F32)).astype(BF16)
    os = [o_ref[:, g * GROUP_WIDTH:(g + 1) * GROUP_WIDTH] for g in range(N_GROUPS)]
    ls = [l_ref[:, g * GROUP_WIDTH:(g + 1) * GROUP_WIDTH] for g in range(N_GROUPS)]
    _post_tail(os, ls, ga_ref, ct, ma_ref, mc_ref, x_ref, wao_ref, wco_ref, wo_ref, gpost_ref, y_ref)


def _weight_specs(layer):
    full2 = (lambda *i: (0, 0))
    slab = (lambda *i: (layer, 0, 0))
    return [
        pl.BlockSpec((None, ATTN_WIDTH, D_MODEL), slab, pipeline_mode=pl.Buffered(1)),
        pl.BlockSpec((None, CONV_WIDTH, D_MODEL), slab, pipeline_mode=pl.Buffered(1)),
        pl.BlockSpec((None, D_MODEL, D_MODEL), slab, pipeline_mode=pl.Buffered(1)),
        pl.BlockSpec((1, D_MODEL), full2),
    ]


def _post_prompt(o, l, ga, ct, ma, mc, x2d, weights, batch, seq, sq, skf, svf, caches_t, layer, prev_states):
    tm = TM_POST
    m = batch * seq
    nt = seq // tm
    db = sq.shape[0]
    depth = caches_t[0].shape[0]
    assert batch * nt == db
    row = lambda b, i: (b * nt + i, 0)
    dec = [pl.BlockSpec((None, tm // dil, dil * GROUP_WIDTH), lambda b, i: (b, i, 0)) for _, dil in GROUPS]
    in_specs = (dec + dec
                + [pl.BlockSpec((tm, ATTN_WIDTH), row),
                   pl.BlockSpec((tm, CONV_WIDTH), row),
                   pl.BlockSpec((tm, D_MODEL), row),
                   pl.BlockSpec((tm, D_MODEL), row),
                   pl.BlockSpec((tm, D_MODEL), row)]
                + _weight_specs(layer))
    assert len(in_specs) == N_POST_IN
    full = pl.BlockSpec((db, ATTN_WIDTH), lambda b, i: (0, 0))
    in_specs += [full] * 3
    out_shape = [jax.ShapeDtypeStruct((m, D_MODEL), F32)] + [jax.ShapeDtypeStruct((db, ATTN_WIDTH), F32)] * 2
    out_specs = [pl.BlockSpec((tm, D_MODEL), row), full, full]
    for c in caches_t:
        length = c.shape[3]
        spec = pl.BlockSpec((None, None, GROUP_WIDTH, length), lambda b, i: (layer, b * nt + i, 0, 0))
        in_specs.append(spec)
        out_shape.append(jax.ShapeDtypeStruct((depth, db, GROUP_WIDTH, length), F32))
        out_specs.append(spec)
    n_alias = len(prev_states)
    in_specs += [pl.BlockSpec(memory_space=pl.ANY)] * n_alias
    aliases = {N_POST_IN + N_SAMPLE_IN + j: 3 + j for j in range(n_alias)}
    outs = pl.pallas_call(
        functools.partial(_post_prompt_kernel, n_alias=n_alias),
        grid=(batch, nt),
        in_specs=in_specs,
        out_specs=out_specs,
        out_shape=out_shape,
        scratch_shapes=[pltpu.VMEM((4, GROUP_WIDTH // LANES, tm, LANES), F32)],
        input_output_aliases=aliases,
        compiler_params=pltpu.CompilerParams(
            dimension_semantics=("arbitrary", "arbitrary"), vmem_limit_bytes=VMEM_LIMIT),
        name="post_prompt",
    )(*o, *l, ga, ct, ma, mc, x2d, *weights, sq, skf, svf, *caches_t, *prev_states)
    return outs[0], outs[1], outs[2], outs[3:9]


def _post_sample(o, l, ga, u, state_t, gc, ma, mc, x2d, weights, conv_w, ln_g, ln_b, layer):
    db = x2d.shape[0]
    row = lambda i: (0, 0)
    in_specs = ([pl.BlockSpec((db, ATTN_WIDTH), row)] * 3
                + [pl.BlockSpec((db, CONV_WIDTH), row),
                   pl.BlockSpec((None, CONV_K - 1, db, CONV_WIDTH), lambda i: (layer, 0, 0, 0)),
                   pl.BlockSpec((db, CONV_WIDTH), row),
                   pl.BlockSpec((db, D_MODEL), row),
                   pl.BlockSpec((db, D_MODEL), row),
                   pl.BlockSpec((db, D_MODEL), row)]
                + _weight_specs(layer)
                + [pl.BlockSpec((CONV_K, CONV_WIDTH), row),
                   pl.BlockSpec((1, CONV_WIDTH), row),
                   pl.BlockSpec((1, CONV_WIDTH), row)])
    return pl.pallas_call(
        _post_sample_kernel,
        grid=(1,),
        in_specs=in_specs,
        out_specs=pl.BlockSpec((db, D_MODEL), row),
        out_shape=jax.ShapeDtypeStruct((db, D_MODEL), F32),
        compiler_params=pltpu.CompilerParams(
            dimension_semantics=("arbitrary",), vmem_limit_bytes=VMEM_LIMIT),
        name="post_sample",
    )(o, l, ga, u, state_t, gc, ma, mc, x2d, *weights, conv_w, ln_g, ln_b)


def _rope_tables(pos):
    inv = jnp.power(jnp.float32(ROPE_THETA), -jnp.arange(0, ROT_DIM, 2, dtype=F32) / ROT_DIM)
    ang = pos.astype(F32)[:, None] * inv[None, :]
    cos, sin = jnp.cos(ang), jnp.sin(ang)
    npos = pos.shape[0]
    half = ROT_DIM // 2
    pad = jnp.zeros((npos, HEAD_DIM - ROT_DIM), F32)
    zero = jnp.zeros((npos, half), F32)
    cos_h = jnp.concatenate([cos, cos, pad + 1.0], axis=1)
    sa_h = jnp.concatenate([-sin, zero, pad], axis=1)
    sb_h = jnp.concatenate([zero, sin, pad], axis=1)
    rep = lambda t: jnp.tile(t, (1, HEADS_PER_GROUP))
    return rep(cos_h), rep(sa_h), rep(sb_h)


def _positions_minor(x):
    depth, n, npos = x.shape[:3]
    return jnp.transpose(x, (0, 1, 3, 4, 2)).reshape(depth, n, GROUP_WIDTH, npos)


def _positions_major(x):
    depth, n, _, npos = x.shape
    return jnp.transpose(x.reshape(depth, n, HEADS_PER_GROUP, HEAD_DIM, npos), (0, 1, 4, 2, 3))


def kernel(x_prompt, x_sample, cache_k0, cache_v0, cache_k1, cache_v1, cache_k2, cache_v2, state_conv,
           w_in, w_attn_out, w_conv_out, w_out, conv_w, conv_ln_g, conv_ln_b, norm_pre, norm_post):
    batch, seq, _ = x_prompt.shape
    db, dseq, _ = x_sample.shape
    depth = w_in.shape[0]
    assert dseq == 1 and seq % TM_IN == 0 and seq >= GROUPS[-1][0]
    caches = (cache_k0, cache_v0, cache_k1, cache_v1, cache_k2, cache_v2)
    for g, (win, _) in enumerate(GROUPS):
        assert caches[2 * g].shape[2] == win and caches[2 * g + 1].shape[2] == win
    caches_t = [_positions_minor(c) for c in caches]
    state_t = jnp.transpose(state_conv, (0, 2, 1, 3))

    w_in_bf = w_in.astype(BF16)
    w_ao_bf = w_attn_out.astype(BF16)
    w_co_bf = w_conv_out.astype(BF16)
    w_o_bf = w_out.astype(BF16)

    tab_p = _rope_tables(jnp.arange(seq, dtype=jnp.int32))
    tab_s = _rope_tables(jnp.full((db,), PAST_LEN, dtype=jnp.int32))

    yp = x_prompt.reshape(batch * seq, D_MODEL)
    ys = x_sample.reshape(db, D_MODEL)
    p_states, s_states = (), ()
    pc, sc = [], []
    for layer in range(depth):
        g_pre = norm_pre[layer][None, :]
        cw = conv_w[layer]
        ln_g = conv_ln_g[layer][None, :]
        ln_b = conv_ln_b[layer][None, :]
        weights = (w_ao_bf, w_co_bf, w_o_bf, norm_post[layer][None, :])

        qkv, p_states, (ga, ct, ma, mc), utail = _inproj_prompt(
            yp, g_pre, w_in_bf, tab_p, cw, ln_g, ln_b, batch, seq, depth, layer, p_states)
        o, l = _attention(qkv, batch, seq)
        (sq, skf, svf), (sga, su, sgc, sma, smc) = _inproj_sample(ys, g_pre, w_in_bf, tab_s, layer)
        yp, so, sl, s_states = _post_prompt(o, l, ga, ct, ma, mc, yp, weights, batch, seq,
                                            sq, skf, svf, caches_t, layer, s_states)
        ys = _post_sample(so, sl, sga, su, state_t, sgc, sma, smc, ys, weights, cw, ln_g, ln_b, layer)
        pc.append(utail[:, CONV_HALO - (CONV_K - 1):])
        sc.append(jnp.concatenate([state_conv[layer][:, 1:], su[:, None, :]], axis=1))

    pk = [_positions_major(p_states[g]) for g in range(N_GROUPS)]
    pv = [_positions_major(p_states[N_GROUPS + g]) for g in range(N_GROUPS)]
    sk = [_positions_major(s_states[2 * g]) for g in range(N_GROUPS)]
    sv = [_positions_major(s_states[2 * g + 1]) for g in range(N_GROUPS)]
    return (yp.reshape(batch, seq, D_MODEL), ys.reshape(db, 1, D_MODEL),
            pk[0], pv[0], pk[1], pv[1], pk[2], pv[2], jnp.stack(pc),
            sk[0], sv[0], sk[1], sv[1], sk[2], sv[2], jnp.stack(sc))
```
